```python
import math
import jax, jax.numpy as jnp
from jax import lax
import numpy as np

D_MODEL = 1024
BATCH = 4
SEQ = 8192
DEPTH = 1

ROPE_THETA = 10000.0
RMS_EPS = 1e-6
NEG_INF = -1e30

MLA_HEADS = 8
MLA_Q_RANK = 256
MLA_KV_RANK = 128
MLA_NOPE_DIM = 64
MLA_ROPE_DIM = 32
MLA_V_DIM = 64
MLA_QK_DIM = MLA_NOPE_DIM + MLA_ROPE_DIM
MLA_WIDTH = MLA_HEADS * MLA_V_DIM
Q_BLOCK = 128

DIL_CONFIGS = ((128, 1), (512, 4), (2048, 16))
DIL_GROUPS = len(DIL_CONFIGS)
DIL_HEADS_PER_GROUP = 8
DIL_HEAD_DIM = 64
DIL_HEADS = DIL_GROUPS * DIL_HEADS_PER_GROUP
DIL_QKV_WIDTH = DIL_HEADS * DIL_HEAD_DIM
DIL_WIDTH = DIL_HEADS_PER_GROUP * DIL_HEAD_DIM

IN_SPLITS = (MLA_Q_RANK, MLA_KV_RANK, MLA_ROPE_DIM, MLA_WIDTH,
             DIL_QKV_WIDTH, DIL_QKV_WIDTH, DIL_QKV_WIDTH, DIL_WIDTH,
             D_MODEL, D_MODEL)
IN_WIDTH = sum(IN_SPLITS)

kernel_name = "hybrid_mla_dilated_gated_encoder"


def rmsnorm(x, g):
    xf = x.astype(jnp.float32)
    xf = xf * lax.rsqrt(jnp.mean(xf * xf, axis=-1, keepdims=True) + RMS_EPS)
    return xf.astype(x.dtype) * g


def rope(t, positions):
    dim = t.shape[-1]
    inv_freq = ROPE_THETA ** (-jnp.arange(0, dim, 2, dtype=jnp.float32) / dim)
    ang = positions.astype(jnp.float32)[..., None] * inv_freq
    ang = jnp.concatenate([ang, ang], axis=-1)[:, :, None, :]
    cos, sin = jnp.cos(ang).astype(t.dtype), jnp.sin(ang).astype(t.dtype)
    t1, t2 = t[..., : dim // 2], t[..., dim // 2:]
    rot = jnp.concatenate([-t2, t1], axis=-1)
    return t * cos + rot * sin


def mla_attention(c_q, c_kv, k_rope, positions, q_norm_g, kv_norm_g, w_uq, w_ukv):
    B, S, _ = c_q.shape
    q = (rmsnorm(c_q, q_norm_g) @ w_uq).reshape(B, S, MLA_HEADS, MLA_QK_DIM)
    q = jnp.concatenate([q[..., :MLA_NOPE_DIM], rope(q[..., MLA_NOPE_DIM:], positions)], axis=-1)
    q = q * (MLA_QK_DIM ** -0.5)
    kv = (rmsnorm(c_kv, kv_norm_g) @ w_ukv).reshape(B, S, MLA_HEADS, MLA_NOPE_DIM + MLA_V_DIM)
    k_nope, v = kv[..., :MLA_NOPE_DIM], kv[..., MLA_NOPE_DIM:]
    k_pe = rope(k_rope[:, :, None, :], positions)
    k = jnp.concatenate([k_nope, jnp.broadcast_to(k_pe, (B, S, MLA_HEADS, MLA_ROPE_DIM))], axis=-1)
    n_blk = S // Q_BLOCK
    qb = q.reshape(B, n_blk, Q_BLOCK, MLA_HEADS, MLA_QK_DIM).transpose(1, 0, 2, 3, 4)

    def one_block(q_blk):
        s = jnp.einsum('bqhd,bkhd->bhqk', q_blk, k).astype(jnp.float32)
        p = jax.nn.softmax(s, axis=-1).astype(v.dtype)
        return jnp.einsum('bhqk,bkhd->bqhd', p, v)

    out = lax.map(one_block, qb)
    return out.transpose(1, 0, 2, 3, 4).reshape(B, S, MLA_WIDTH)


def dilated_group(q, k, v, dilation, side):
    B, S, H, D = q.shape
    L = S // dilation
    nb = -(-L // side)
    Lp = nb * side

    def to_sub(t):
        t = t.reshape(B, L, dilation, H, D).transpose(0, 2, 3, 1, 4)
        return jnp.pad(t, ((0, 0), (0, 0), (0, 0), (0, Lp - L), (0, 0)))

    def neighbourhood(t):
        tp = jnp.pad(t, ((0, 0), (0, 0), (0, 0), (side, side), (0, 0)))
        tb = tp.reshape(B, dilation, H, nb + 2, side, D)
        return jnp.concatenate([tb[:, :, :, :-2], tb[:, :, :, 1:-1], tb[:, :, :, 2:]], axis=4)

    qb = to_sub(q).reshape(B, dilation, H, nb, side, D)
    kb = neighbourhood(to_sub(k))
    vb = neighbourhood(to_sub(v))
    a = jnp.arange(side)[:, None]
    c = jnp.arange(3 * side)[None, :]
    key_pos = (jnp.arange(nb)[:, None, None] - 1) * side + c[None]
    valid = (c >= a)[None] & (c <= a + 2 * side)[None] & (key_pos >= 0) & (key_pos < L)
    s = jnp.einsum('brhnqd,brhnkd->brhnqk', qb, kb).astype(jnp.float32) * (D ** -0.5)
    s = jnp.where(valid, s, NEG_INF)
    m = jnp.max(s, axis=-1, keepdims=True)
    p = jnp.exp(s - m)
    den = jnp.sum(p, axis=-1)
    o = jnp.einsum('brhnqk,brhnkd->brhnqd', p.astype(v.dtype), vb).astype(jnp.float32) / den[..., None]
    lse = m[..., 0] + jnp.log(den)
    o = o.reshape(B, dilation, H, Lp, D)[:, :, :, :L].transpose(0, 3, 1, 2, 4).reshape(B, S, H, D)
    lse = lse.reshape(B, dilation, H, Lp)[:, :, :, :L].transpose(0, 3, 1, 2).reshape(B, S, H)
    return o, lse


def dilated_mixture(q, k, v, positions):
    B, S, _ = q.shape
    q = rope(q.reshape(B, S, DIL_HEADS, DIL_HEAD_DIM), positions)
    k = rope(k.reshape(B, S, DIL_HEADS, DIL_HEAD_DIM), positions)
    v = v.reshape(B, S, DIL_HEADS, DIL_HEAD_DIM)
    outs, lses = [], []
    for g, (window, dilation) in enumerate(DIL_CONFIGS):
        sl = slice(g * DIL_HEADS_PER_GROUP, (g + 1) * DIL_HEADS_PER_GROUP)
        o, lse = dilated_group(q[:, :, sl], k[:, :, sl], v[:, :, sl], dilation, window // (2 * dilation))
        outs.append(o)
        lses.append(lse)
    outs = jnp.stack(outs, axis=0)
    alpha = jax.nn.softmax(jnp.stack(lses, axis=0), axis=0)
    out = jnp.sum(alpha[..., None] * outs, axis=0)
    return out.reshape(B, S, DIL_WIDTH).astype(q.dtype)


def hybrid_layer(x, positions, attn_norm_g, w_in, b_gate, mla_q_norm_g, mla_kv_norm_g,
                 w_uq, w_ukv, w_o_mla, w_o_dil, w_out):
    h = rmsnorm(x, attn_norm_g)
    proj = h @ w_in
    split_points = [int(p) for p in np.cumsum(IN_SPLITS)[:-1]]
    (c_q, c_kv, k_rope, z_mla, q_dil, k_dil, v_dil, z_dil,
     g_mla, g_dil) = jnp.split(proj, split_points, axis=-1)
    a_out = mla_attention(c_q, c_kv, k_rope, positions, mla_q_norm_g, mla_kv_norm_g, w_uq, w_ukv)
    y_mla = (a_out * jax.nn.silu(z_mla)) @ w_o_mla
    d_out = dilated_mixture(q_dil, k_dil, v_dil, positions)
    y_dil = (d_out * jax.nn.silu(z_dil)) @ w_o_dil
    gate_mla = jax.nn.sigmoid(g_mla + b_gate[:D_MODEL])
    gate_dil = jax.nn.sigmoid(g_dil + b_gate[D_MODEL:])
    merged = gate_mla * y_mla + gate_dil * y_dil
    return x + merged @ w_out


def setup_inputs(seed: int = 0) -> dict:
    key = jax.random.key(seed)
    ks = jax.random.split(key, 14)
    f32 = jnp.float32

    def normal(k, shape, fan_in):
        return jax.random.normal(k, shape, f32) * fan_in ** -0.5

    def gain(k, shape):
        return 1.0 + 0.02 * jax.random.normal(k, shape, f32)

    x = jax.random.normal(ks[0], (BATCH, SEQ, D_MODEL), f32)
    positions = jnp.broadcast_to(jnp.arange(SEQ, dtype=jnp.int32)[None, :], (BATCH, SEQ))
    return {
        "x": x,
        "positions": positions,
        "attn_norm_g": gain(ks[1], (DEPTH, D_MODEL)),
        "w_in": normal(ks[2], (DEPTH, D_MODEL, IN_WIDTH), D_MODEL),
        "b_gate": 0.02 * jax.random.normal(ks[3], (DEPTH, 2 * D_MODEL), f32),
        "mla_q_norm_g": gain(ks[4], (DEPTH, MLA_Q_RANK)),
        "mla_kv_norm_g": gain(ks[5], (DEPTH, MLA_KV_RANK)),
        "w_uq": normal(ks[6], (DEPTH, MLA_Q_RANK, MLA_HEADS * MLA_QK_DIM), MLA_Q_RANK),
        "w_ukv": normal(ks[7], (DEPTH, MLA_KV_RANK, MLA_HEADS * (MLA_NOPE_DIM + MLA_V_DIM)), MLA_KV_RANK),
        "w_o_mla": normal(ks[8], (DEPTH, MLA_WIDTH, D_MODEL), MLA_WIDTH),
        "w_o_dil": normal(ks[9], (DEPTH, DIL_WIDTH, D_MODEL), DIL_WIDTH),
        "w_out": normal(ks[10], (DEPTH, D_MODEL, D_MODEL), D_MODEL),
        "final_norm_g": gain(ks[11], (D_MODEL,)),
    }


def reference(x, positions, attn_norm_g, w_in, b_gate, mla_q_norm_g, mla_kv_norm_g,
              w_uq, w_ukv, w_o_mla, w_o_dil, w_out, final_norm_g):
    h = x
    for layer in range(DEPTH):
        h = hybrid_layer(h, positions, attn_norm_g[layer], w_in[layer], b_gate[layer],
                         mla_q_norm_g[layer], mla_kv_norm_g[layer], w_uq[layer], w_ukv[layer],
                         w_o_mla[layer], w_o_dil[layer], w_out[layer])
    return rmsnorm(h, final_norm_g)
```

```python
import functools

import jax
import jax.numpy as jnp
from jax import lax
from jax.experimental import pallas as pl
from jax.experimental.pallas import tpu as pltpu

D_MODEL = 1024
ROPE_THETA = 10000.0
RMS_EPS = 1e-6
NEG_INF = -1e30

MLA_HEADS = 8
MLA_Q_RANK = 256
MLA_KV_RANK = 128
MLA_NOPE_DIM = 64
MLA_ROPE_DIM = 32
MLA_V_DIM = 64
MLA_QK_DIM = MLA_NOPE_DIM + MLA_ROPE_DIM
MLA_WIDTH = MLA_HEADS * MLA_V_DIM

DIL_CONFIGS = ((128, 1), (512, 4), (2048, 16))
DIL_GROUPS = len(DIL_CONFIGS)
DIL_HEADS_PER_GROUP = 8
DIL_HEAD_DIM = 64
DIL_QKV_WIDTH = DIL_GROUPS * DIL_HEADS_PER_GROUP * DIL_HEAD_DIM
DIL_WIDTH = DIL_HEADS_PER_GROUP * DIL_HEAD_DIM

IN_SPLITS = (MLA_Q_RANK, MLA_KV_RANK, MLA_ROPE_DIM, MLA_WIDTH,
             DIL_QKV_WIDTH, DIL_QKV_WIDTH, DIL_QKV_WIDTH, DIL_WIDTH,
             D_MODEL, D_MODEL)

LANES = 128
MLA_HEAD_PAD = LANES
VMEM_LIMIT_BYTES = 56 * 1024 * 1024

BF16 = jnp.bfloat16
F32 = jnp.float32


def _dot(a, b):
    return jnp.dot(a, b, preferred_element_type=F32)


def _rms(x, g):
    return x * lax.rsqrt(jnp.mean(x * x, axis=-1, keepdims=True) + RMS_EPS) * g


def _rope_tile(x, cos, sin_signed, first_half, half):
    rot = jnp.where(first_half, pltpu.roll(x, LANES - half, axis=1), pltpu.roll(x, half, axis=1))
    return x * cos + rot * sin_signed


def _proj_kernel(x_ref, pos_ref, gattn_ref, wmla_ref, wdil_ref, gq_ref, gkv_ref,
                 wuq_ref, wuk_ref, wuv_ref, tab_ref,
                 q_ref, k_ref, v_ref, qd_ref, kd_ref, vd_ref):
    x = x_ref[...]
    hb = _rms(x, gattn_ref[...]).astype(BF16)
    pos = pos_ref[...].astype(F32)

    tab = tab_ref[...]
    lane = lax.broadcasted_iota(jnp.int32, (1, LANES), 1)

    ang_m = pos * tab[0:1, :]
    cos_m, sin_m = jnp.cos(ang_m), jnp.sin(ang_m) * tab[1:2, :]
    first_m = lane < (MLA_NOPE_DIM + MLA_ROPE_DIM // 2)
    q_scale = MLA_QK_DIM ** -0.5

    mla = _dot(hb, wmla_ref[...])
    c_q = mla[:, :MLA_Q_RANK]
    c_kv = mla[:, MLA_Q_RANK:MLA_Q_RANK + MLA_KV_RANK]
    kpe = mla[:, MLA_Q_RANK + MLA_KV_RANK:]
    kpe = _rope_tile(kpe, cos_m, sin_m, first_m, MLA_ROPE_DIM // 2)

    cqn = _rms(c_q, gq_ref[...]).astype(BF16)
    ckn = _rms(c_kv, gkv_ref[...]).astype(BF16)
    q = _dot(cqn, wuq_ref[...])
    kn = _dot(ckn, wuk_ref[...])
    v_ref[...] = _dot(ckn, wuv_ref[...]).astype(BF16)
    cos_q, sin_q = cos_m * q_scale, sin_m * q_scale
    for h in range(MLA_HEADS):
        sl = slice(h * MLA_HEAD_PAD, (h + 1) * MLA_HEAD_PAD)
        q_ref[:, sl] = _rope_tile(q[:, sl], cos_q, sin_q, first_m, MLA_ROPE_DIM // 2).astype(BF16)
        k_ref[:, sl] = (kn[:, sl] + kpe).astype(BF16)

    ang_d = pos * tab[2:3, :]
    cos_d, sin_d = jnp.cos(ang_d), jnp.sin(ang_d) * tab[3:4, :]
    first_d = (lane % DIL_HEAD_DIM) < (DIL_HEAD_DIM // 2)
    n_tiles = DIL_QKV_WIDTH // LANES
    for part, out_ref in enumerate((qd_ref, kd_ref, vd_ref)):
        w = wdil_ref[:, part * DIL_QKV_WIDTH:(part + 1) * DIL_QKV_WIDTH]
        y = _dot(hb, w)
        if part == 2:
            out_ref[...] = y.astype(BF16)
        else:
            for t in range(n_tiles):
                sl = slice(t * LANES, (t + 1) * LANES)
                out_ref[:, sl] = _rope_tile(y[:, sl], cos_d, sin_d, first_d,
                                            DIL_HEAD_DIM // 2).astype(BF16)


def _rope_tables():
    lane = jnp.arange(LANES)
    inv_m = ROPE_THETA ** (-jnp.arange(0, MLA_ROPE_DIM, 2, dtype=F32) / MLA_ROPE_DIM)
    inv_d = ROPE_THETA ** (-jnp.arange(0, DIL_HEAD_DIM, 2, dtype=F32) / DIL_HEAD_DIM)
    in_rope = (lane >= MLA_NOPE_DIM) & (lane < MLA_QK_DIM)
    r = lane - MLA_NOPE_DIM
    tab_m = jnp.where(in_rope, inv_m[jnp.clip(r, 0, MLA_ROPE_DIM - 1) % (MLA_ROPE_DIM // 2)], 0.0)
    sgn_m = jnp.where(in_rope, jnp.where(r < MLA_ROPE_DIM // 2, -1.0, 1.0), 0.0)
    e = lane % DIL_HEAD_DIM
    tab_d = inv_d[e % (DIL_HEAD_DIM // 2)]
    sgn_d = jnp.where(e < DIL_HEAD_DIM // 2, -1.0, 1.0)
    zeros = jnp.zeros((LANES,), F32)
    return jnp.stack([tab_m, sgn_m, tab_d, sgn_d, zeros, zeros, zeros, zeros]).astype(F32)


def _projection(x2, pos2, attn_norm_g, w_in, gq, gkv, w_uq, w_ukv, *, tm):
    T = x2.shape[0]
    o = [0]
    for s in IN_SPLITS:
        o.append(o[-1] + s)
    w_cq, w_ckv, w_kr = w_in[:, o[0]:o[1]], w_in[:, o[1]:o[2]], w_in[:, o[2]:o[3]]
    kpe_slot = jnp.zeros((D_MODEL, LANES), F32).at[:, MLA_NOPE_DIM:MLA_QK_DIM].set(w_kr)
    w_mla = jnp.concatenate([w_cq, w_ckv, kpe_slot], axis=1).astype(BF16)
    w_dil = w_in[:, o[4]:o[7]].astype(BF16)
    wq = w_uq.reshape(MLA_Q_RANK, MLA_HEADS, MLA_QK_DIM)
    wq = jnp.pad(wq, ((0, 0), (0, 0), (0, MLA_HEAD_PAD - MLA_QK_DIM)))
    wq = wq.reshape(MLA_Q_RANK, MLA_HEADS * MLA_HEAD_PAD).astype(BF16)
    wkv = w_ukv.reshape(MLA_KV_RANK, MLA_HEADS, MLA_NOPE_DIM + MLA_V_DIM)
    wk = jnp.pad(wkv[:, :, :MLA_NOPE_DIM], ((0, 0), (0, 0), (0, MLA_HEAD_PAD - MLA_NOPE_DIM)))
    wk = wk.reshape(MLA_KV_RANK, MLA_HEADS * MLA_HEAD_PAD).astype(BF16)
    wv = wkv[:, :, MLA_NOPE_DIM:].reshape(MLA_KV_RANK, MLA_WIDTH).astype(BF16)

    full = lambda a: pl.BlockSpec(a.shape, lambda i: (0,) * a.ndim)
    rows = lambda w: pl.BlockSpec((tm, w), lambda i: (i, 0))
    ins = [x2, pos2, attn_norm_g.reshape(1, -1), w_mla, w_dil, gq.reshape(1, -1), gkv.reshape(1, -1),
           wq, wk, wv, _rope_tables()]
    in_specs = [rows(D_MODEL), rows(1)] + [full(a) for a in ins[2:]]
    qk_w = MLA_HEADS * MLA_HEAD_PAD
    widths = (qk_w, qk_w, MLA_WIDTH, DIL_QKV_WIDTH, DIL_QKV_WIDTH, DIL_QKV_WIDTH)
    return pl.pallas_call(
        _proj_kernel,
        grid=(T // tm,),
        in_specs=in_specs,
        out_specs=[rows(w) for w in widths],
        out_shape=[jax.ShapeDtypeStruct((T, w), BF16) for w in widths],
        compiler_params=pltpu.CompilerParams(dimension_semantics=("parallel",),
                                             vmem_limit_bytes=VMEM_LIMIT_BYTES),
        name="proj",
    )(*ins)


def _mla_attn_kernel(qT_ref, k_ref, vT_ref, oT_ref, *, tk):
    qT = qT_ref[...]
    tq = qT.shape[1]
    nk = k_ref.shape[0] // tk

    def body(i, carry):
        m, l, acc = carry
        kc = k_ref[pl.ds(pl.multiple_of(i * tk, tk), tk), :]
        s = _dot(kc, qT)
        m_new = jnp.maximum(m, jnp.max(s, axis=0, keepdims=True))
        alpha = jnp.exp(m - m_new)
        p = jnp.exp(s - m_new)
        l = alpha * l + jnp.sum(p, axis=0, keepdims=True)
        acc = alpha * acc + _dot(vT_ref[i], p.astype(BF16))
        return m_new, l, acc

    init = (jnp.full((1, tq), -jnp.inf, F32), jnp.zeros((1, tq), F32),
            jnp.zeros((MLA_V_DIM, tq), F32))
    _, l, acc = lax.fori_loop(0, nk, body, init)
    oT_ref[...] = (acc / l).astype(oT_ref.dtype)


def _mla_attention(qT, k, vT, *, tq, tk):
    B, _, S = qT.shape
    nk = S // tk
    return pl.pallas_call(
        functools.partial(_mla_attn_kernel, tk=tk),
        grid=(B, MLA_HEADS, S // tq),
        in_specs=[
            pl.BlockSpec((None, MLA_HEAD_PAD, tq), lambda b, h, i: (b, h, i)),
            pl.BlockSpec((None, S, MLA_HEAD_PAD), lambda b, h, i: (b, 0, h)),
            pl.BlockSpec((None, None, nk, MLA_V_DIM, tk), lambda b, h, i: (b, h, 0, 0, 0)),
        ],
        out_specs=pl.BlockSpec((None, MLA_V_DIM, tq), lambda b, h, i: (b, h, i)),
        out_shape=jax.ShapeDtypeStruct((B, MLA_WIDTH, S), BF16),
        compiler_params=pltpu.CompilerParams(
            dimension_semantics=("parallel", "parallel", "parallel"),
            vmem_limit_bytes=VMEM_LIMIT_BYTES),
        name="mla_attn",
    )(qT, k, vT)


def _dil_attn_kernel(q_ref, k_ref, v_ref, o_ref, lse_ref, *, tl, side):
    L = k_ref.shape[0]
    win = tl + 2 * side
    n = pl.program_id(2)
    q0 = n * tl
    ks = pl.multiple_of(jnp.clip(q0 - side, 0, L - win), side)
    a = lax.broadcasted_iota(jnp.int32, (tl, win), 0)
    c = lax.broadcasted_iota(jnp.int32, (tl, win), 1)
    valid = jnp.abs((a - c) + (q0 - ks)) <= side
    lane = lax.broadcasted_iota(jnp.int32, (1, LANES), 1)
    lo = lane < DIL_HEAD_DIM
    scale = DIL_HEAD_DIM ** -0.5
    for pair in range(DIL_HEADS_PER_GROUP // 2):
        sl = slice(pair * LANES, (pair + 1) * LANES)
        qp = q_ref[:, sl]
        kp = k_ref[pl.ds(ks, win), sl]
        vp = v_ref[pl.ds(ks, win), sl]
        outs, lses = [], []
        for half in (lo, ~lo):
            qh = jnp.where(half, qp, jnp.zeros_like(qp))
            s = lax.dot_general(qh, kp, (((1,), (1,)), ((), ())),
                                preferred_element_type=F32) * scale
            s = jnp.where(valid, s, NEG_INF)
            m = jnp.max(s, axis=-1, keepdims=True)
            p = jnp.exp(s - m)
            den = jnp.sum(p, axis=-1, keepdims=True)
            outs.append(_dot(p.astype(BF16), vp) / den)
            lses.append(m + jnp.log(den))
        o_ref[:, sl] = jnp.where(lo, outs[0], outs[1]).astype(o_ref.dtype)
        lse_ref[:, sl] = jnp.where(lo, lses[0], lses[1])


def _dilated_group(qd, kd, vd, g, dilation, side, *, tl):
    B, S, _ = qd.shape
    L = S // dilation
    tl = min(tl, L - 2 * side)
    view = lambda a: a.reshape(B, L, dilation * DIL_QKV_WIDTH)
    col = lambda r: r * DIL_GROUPS + g
    q_spec = pl.BlockSpec((None, tl, DIL_WIDTH), lambda b, r, n: (b, n, col(r)))
    kv_spec = pl.BlockSpec((None, L, DIL_WIDTH), lambda b, r, n: (b, 0, col(r)),
                           pipeline_mode=pl.Buffered(1))
    out_spec = pl.BlockSpec((None, tl, DIL_WIDTH), lambda b, r, n: (b, n, r))
    o, lse = pl.pallas_call(
        functools.partial(_dil_attn_kernel, tl=tl, side=side),
        grid=(B, dilation, L // tl),
        in_specs=[q_spec, kv_spec, kv_spec],
        out_specs=[out_spec, out_spec],
        out_shape=[jax.ShapeDtypeStruct((B, L, dilation * DIL_WIDTH), BF16),
                   jax.ShapeDtypeStruct((B, L, dilation * DIL_WIDTH), F32)],
        compiler_params=pltpu.CompilerParams(
            dimension_semantics=("parallel", "parallel", "arbitrary"),
            vmem_limit_bytes=VMEM_LIMIT_BYTES),
        name=f"dil_attn_g{g}",
    )(view(qd), view(kd), view(vd))
    return o.reshape(B * S, DIL_WIDTH), lse.reshape(B * S, DIL_WIDTH)


def _out_kernel(x_ref, a_ref, o0_ref, o1_ref, o2_ref, l0_ref, l1_ref, l2_ref,
                gattn_ref, wzg_ref, bg_ref, womla_ref, wodil_ref, wout_ref, gfin_ref, out_ref,
                *, final_norm):
    x = x_ref[...]
    hb = _rms(x, gattn_ref[...]).astype(BF16)
    zg = _dot(hb, wzg_ref[...])
    z_mla = zg[:, :MLA_WIDTH]
    z_dil = zg[:, MLA_WIDTH:MLA_WIDTH + DIL_WIDTH]
    g = zg[:, MLA_WIDTH + DIL_WIDTH:] + bg_ref[...]
    g_mla, g_dil = g[:, :D_MODEL], g[:, D_MODEL:]

    y_mla = _dot((a_ref[...].astype(F32) * jax.nn.silu(z_mla)).astype(BF16), womla_ref[...])

    l0, l1, l2 = l0_ref[...], l1_ref[...], l2_ref[...]
    mx = jnp.maximum(jnp.maximum(l0, l1), l2)
    e0, e1, e2 = jnp.exp(l0 - mx), jnp.exp(l1 - mx), jnp.exp(l2 - mx)
    den = e0 + e1 + e2
    d = (e0 / den * o0_ref[...].astype(F32) + e1 / den * o1_ref[...].astype(F32)
         + e2 / den * o2_ref[...].astype(F32))
    y_dil = _dot((d * jax.nn.silu(z_dil)).astype(BF16), wodil_ref[...])

    merged = jax.nn.sigmoid(g_mla) * y_mla + jax.nn.sigmoid(g_dil) * y_dil
    y = x + _dot(merged.astype(BF16), wout_ref[...])
    out_ref[...] = _rms(y, gfin_ref[...]) if final_norm else y


def _output(x2, a_out, os, lses, attn_norm_g, w_in, b_gate, w_o_mla, w_o_dil, w_out, final_norm_g, *,
            tm, final_norm):
    T = x2.shape[0]
    o = [0]
    for s in IN_SPLITS:
        o.append(o[-1] + s)
    w_zg = jnp.concatenate([w_in[:, o[3]:o[4]], w_in[:, o[7]:o[10]]], axis=1).astype(BF16)
    full = lambda a: pl.BlockSpec(a.shape, lambda i: (0,) * a.ndim)
    rows = lambda w: pl.BlockSpec((tm, w), lambda i: (i, 0))
    params = [attn_norm_g.reshape(1, -1), w_zg, b_gate.reshape(1, -1), w_o_mla.astype(BF16),
              w_o_dil.astype(BF16), w_out.astype(BF16), final_norm_g.reshape(1, -1)]
    return pl.pallas_call(
        functools.partial(_out_kernel, final_norm=final_norm),
        grid=(T // tm,),
        in_specs=[rows(D_MODEL)] + [rows(DIL_WIDTH)] * 7 + [full(a) for a in params],
        out_specs=rows(D_MODEL),
        out_shape=jax.ShapeDtypeStruct((T, D_MODEL), F32),
        compiler_params=pltpu.CompilerParams(dimension_semantics=("parallel",),
                                             vmem_limit_bytes=VMEM_LIMIT_BYTES),
        name="out",
    )(x2, a_out, *os, *lses, *params)


def kernel(x, positions, attn_norm_g, w_in, b_gate, mla_q_norm_g, mla_kv_norm_g, w_uq, w_ukv,
           w_o_mla, w_o_dil, w_out, final_norm_g):
    B, S, _ = x.shape
    depth = w_in.shape[0]
    h = x.reshape(B * S, D_MODEL)
    pos2 = positions.reshape(B * S, 1)
    tk = 512
    for layer in range(depth):
        q, k, v, qd, kd, vd = _projection(
            h, pos2, attn_norm_g[layer], w_in[layer], mla_q_norm_g[layer], mla_kv_norm_g[layer],
            w_uq[layer], w_ukv[layer], tm=256)
        qk_w = MLA_HEADS * MLA_HEAD_PAD
        qT = q.reshape(B, S, qk_w).transpose(0, 2, 1)
        vT = v.reshape(B, S // tk, tk, MLA_HEADS, MLA_V_DIM).transpose(0, 3, 1, 4, 2)
        oT = _mla_attention(qT, k.reshape(B, S, qk_w), vT, tq=512, tk=tk)
        a_out = oT.transpose(0, 2, 1).reshape(B * S, MLA_WIDTH)

        shp = (B, S, DIL_QKV_WIDTH)
        os, lses = [], []
        for g, (window, dilation) in enumerate(DIL_CONFIGS):
            o_g, lse_g = _dilated_group(qd.reshape(shp), kd.reshape(shp), vd.reshape(shp),
                                        g, dilation, window // (2 * dilation), tl=256)
            os.append(o_g)
            lses.append(lse_g)
        h = _output(h, a_out, os, lses, attn_norm_g[layer], w_in[layer], b_gate[layer],
                    w_o_mla[layer], w_o_dil[layer], w_out[layer], final_norm_g, tm=256,
                    final_norm=(layer == depth - 1))
    return h.reshape(B, S, D_MODEL)
```

```python
import functools

import jax
import jax.numpy as jnp
from jax import lax
from jax.experimental import pallas as pl
from jax.experimental.pallas import tpu as pltpu

D_MODEL = 1024
ROPE_THETA = 10000.0
RMS_EPS = 1e-6
NEG_INF = -1e30

MLA_HEADS = 8
MLA_Q_RANK = 256
MLA_KV_RANK = 128
MLA_NOPE_DIM = 64
MLA_ROPE_DIM = 32
MLA_V_DIM = 64
MLA_QK_DIM = MLA_NOPE_DIM + MLA_ROPE_DIM
MLA_WIDTH = MLA_HEADS * MLA_V_DIM

DIL_CONFIGS = ((128, 1), (512, 4), (2048, 16))
DIL_GROUPS = len(DIL_CONFIGS)
DIL_HEADS_PER_GROUP = 8
DIL_HEAD_DIM = 64
DIL_QKV_WIDTH = DIL_GROUPS * DIL_HEADS_PER_GROUP * DIL_HEAD_DIM
DIL_WIDTH = DIL_HEADS_PER_GROUP * DIL_HEAD_DIM

IN_SPLITS = (MLA_Q_RANK, MLA_KV_RANK, MLA_ROPE_DIM, MLA_WIDTH,
             DIL_QKV_WIDTH, DIL_QKV_WIDTH, DIL_QKV_WIDTH, DIL_WIDTH,
             D_MODEL, D_MODEL)

LANES = 128
MLA_HEAD_PAD = LANES
VMEM_LIMIT_BYTES = 56 * 1024 * 1024

BF16 = jnp.bfloat16
F32 = jnp.float32
LOG2_E = 1.4426950408889634


def _dot(a, b):
    return jnp.dot(a, b, preferred_element_type=F32)


def _rms(x, g):
    return x * lax.rsqrt(jnp.mean(x * x, axis=-1, keepdims=True) + RMS_EPS) * g


def _rope_tile(x, cos, sin_signed, first_half, half):
    rot = jnp.where(first_half, pltpu.roll(x, LANES - half, axis=1), pltpu.roll(x, half, axis=1))
    return x * cos + rot * sin_signed


def _proj_kernel(x_ref, pos_ref, gattn_ref, wmla_ref, wdil_ref, gq_ref, gkv_ref,
                 wuq_ref, wuk_ref, wuv_ref, tab_ref,
                 q_ref, k_ref, v_ref, qd_ref, kd_ref, vd_ref):
    x = x_ref[...]
    hb = _rms(x, gattn_ref[...]).astype(BF16)
    pos = pos_ref[...].astype(F32)

    tab = tab_ref[...]
    lane = lax.broadcasted_iota(jnp.int32, (1, LANES), 1)

    ang_m = pos * tab[0:1, :]
    cos_m, sin_m = jnp.cos(ang_m), jnp.sin(ang_m) * tab[1:2, :]
    first_m = lane < (MLA_NOPE_DIM + MLA_ROPE_DIM // 2)
    q_scale = MLA_QK_DIM ** -0.5 * LOG2_E

    mla = _dot(hb, wmla_ref[...])
    c_q = mla[:, :MLA_Q_RANK]
    c_kv = mla[:, MLA_Q_RANK:MLA_Q_RANK + MLA_KV_RANK]
    kpe = mla[:, MLA_Q_RANK + MLA_KV_RANK:]
    kpe = _rope_tile(kpe, cos_m, sin_m, first_m, MLA_ROPE_DIM // 2)

    cqn = _rms(c_q, gq_ref[...]).astype(BF16)
    ckn = _rms(c_kv, gkv_ref[...]).astype(BF16)
    q = _dot(cqn, wuq_ref[...])
    kn = _dot(ckn, wuk_ref[...])
    v_ref[...] = _dot(ckn, wuv_ref[...]).astype(BF16)
    cos_q, sin_q = cos_m * q_scale, sin_m * q_scale
    for h in range(MLA_HEADS):
        sl = slice(h * MLA_HEAD_PAD, (h + 1) * MLA_HEAD_PAD)
        q_ref[:, sl] = _rope_tile(q[:, sl], cos_q, sin_q, first_m, MLA_ROPE_DIM // 2).astype(BF16)
        k_ref[:, sl] = (kn[:, sl] + kpe).astype(BF16)

    ang_d = pos * tab[2:3, :]
    cos_d, sin_d = jnp.cos(ang_d), jnp.sin(ang_d) * tab[3:4, :]
    first_d = (lane % DIL_HEAD_DIM) < (DIL_HEAD_DIM // 2)
    n_tiles = DIL_QKV_WIDTH // LANES
    for part, out_ref in enumerate((qd_ref, kd_ref, vd_ref)):
        w = wdil_ref[:, part * DIL_QKV_WIDTH:(part + 1) * DIL_QKV_WIDTH]
        y = _dot(hb, w)
        if part == 2:
            out_ref[...] = y.astype(BF16)
        else:
            for t in range(n_tiles):
                sl = slice(t * LANES, (t + 1) * LANES)
                out_ref[:, sl] = _rope_tile(y[:, sl], cos_d, sin_d, first_d,
                                            DIL_HEAD_DIM // 2).astype(BF16)


def _rope_tables():
    lane = jnp.arange(LANES)
    inv_m = ROPE_THETA ** (-jnp.arange(0, MLA_ROPE_DIM, 2, dtype=F32) / MLA_ROPE_DIM)
    inv_d = ROPE_THETA ** (-jnp.arange(0, DIL_HEAD_DIM, 2, dtype=F32) / DIL_HEAD_DIM)
    in_rope = (lane >= MLA_NOPE_DIM) & (lane < MLA_QK_DIM)
    r = lane - MLA_NOPE_DIM
    tab_m = jnp.where(in_rope, inv_m[jnp.clip(r, 0, MLA_ROPE_DIM - 1) % (MLA_ROPE_DIM // 2)], 0.0)
    sgn_m = jnp.where(in_rope, jnp.where(r < MLA_ROPE_DIM // 2, -1.0, 1.0), 0.0)
    e = lane % DIL_HEAD_DIM
    tab_d = inv_d[e % (DIL_HEAD_DIM // 2)]
    sgn_d = jnp.where(e < DIL_HEAD_DIM // 2, -1.0, 1.0)
    zeros = jnp.zeros((LANES,), F32)
    return jnp.stack([tab_m, sgn_m, tab_d, sgn_d, zeros, zeros, zeros, zeros]).astype(F32)


def _projection(x2, pos2, attn_norm_g, w_in, gq, gkv, w_uq, w_ukv, *, tm):
    T = x2.shape[0]
    o = [0]
    for s in IN_SPLITS:
        o.append(o[-1] + s)
    w_cq, w_ckv, w_kr = w_in[:, o[0]:o[1]], w_in[:, o[1]:o[2]], w_in[:, o[2]:o[3]]
    kpe_slot = jnp.zeros((D_MODEL, LANES), F32).at[:, MLA_NOPE_DIM:MLA_QK_DIM].set(w_kr)
    w_mla = jnp.concatenate([w_cq, w_ckv, kpe_slot], axis=1).astype(BF16)
    w_dil = w_in[:, o[4]:o[7]].astype(BF16)
    wq = w_uq.reshape(MLA_Q_RANK, MLA_HEADS, MLA_QK_DIM)
    wq = jnp.pad(wq, ((0, 0), (0, 0), (0, MLA_HEAD_PAD - MLA_QK_DIM)))
    wq = wq.reshape(MLA_Q_RANK, MLA_HEADS * MLA_HEAD_PAD).astype(BF16)
    wkv = w_ukv.reshape(MLA_KV_RANK, MLA_HEADS, MLA_NOPE_DIM + MLA_V_DIM)
    wk = jnp.pad(wkv[:, :, :MLA_NOPE_DIM], ((0, 0), (0, 0), (0, MLA_HEAD_PAD - MLA_NOPE_DIM)))
    wk = wk.reshape(MLA_KV_RANK, MLA_HEADS * MLA_HEAD_PAD).astype(BF16)
    wv = wkv[:, :, MLA_NOPE_DIM:].reshape(MLA_KV_RANK, MLA_WIDTH).astype(BF16)

    full = lambda a: pl.BlockSpec(a.shape, lambda i: (0,) * a.ndim)
    rows = lambda w: pl.BlockSpec((tm, w), lambda i: (i, 0))
    ins = [x2, pos2, attn_norm_g.reshape(1, -1), w_mla, w_dil, gq.reshape(1, -1), gkv.reshape(1, -1),
           wq, wk, wv, _rope_tables()]
    in_specs = [rows(D_MODEL), rows(1)] + [full(a) for a in ins[2:]]
    qk_w = MLA_HEADS * MLA_HEAD_PAD
    widths = (qk_w, qk_w, MLA_WIDTH, DIL_QKV_WIDTH, DIL_QKV_WIDTH, DIL_QKV_WIDTH)
    return pl.pallas_call(
        _proj_kernel,
        grid=(T // tm,),
        in_specs=in_specs,
        out_specs=[rows(w) for w in widths],
        out_shape=[jax.ShapeDtypeStruct((T, w), BF16) for w in widths],
        compiler_params=pltpu.CompilerParams(dimension_semantics=("parallel",),
                                             vmem_limit_bytes=VMEM_LIMIT_BYTES),
        name="proj",
    )(*ins)


def _mla_attn_kernel(qT_ref, k_ref, vT_ref, oT_ref, s0_ref, s1_ref, *, tk):
    qT = qT_ref[...]
    tq = qT.shape[1]
    nk = k_ref.shape[0] // tk

    def scores(j):
        kc = k_ref[pl.ds(pl.multiple_of(j * tk, tk), tk), :]
        return _dot(kc, qT)

    def update(s_ref, j, carry):
        m, l, acc = carry
        s = s_ref[...]
        m_new = jnp.maximum(m, jnp.max(s, axis=0, keepdims=True))
        alpha = jnp.exp2(m - m_new)
        p = jnp.exp2(s - m_new)
        l = alpha * l + jnp.sum(p, axis=0, keepdims=True)
        acc = alpha * acc + _dot(vT_ref[j], p.astype(BF16))
        return m_new, l, acc

    def body(t, carry):
        j = 2 * t
        s1_ref[...] = scores(j + 1)
        carry = update(s0_ref, j, carry)
        s0_ref[...] = scores(j + 2)
        return update(s1_ref, j + 1, carry)

    carry = (jnp.full((1, tq), -jnp.inf, F32), jnp.zeros((1, tq), F32),
             jnp.zeros((MLA_V_DIM, tq), F32))
    s0_ref[...] = scores(0)
    carry = lax.fori_loop(0, nk // 2 - 1, body, carry)
    s1_ref[...] = scores(nk - 1)
    carry = update(s0_ref, nk - 2, carry)
    _, l, acc = update(s1_ref, nk - 1, carry)
    oT_ref[...] = (acc / l).astype(oT_ref.dtype)


def _mla_attention(qT, k, vT, *, tq, tk):
    B, _, S = qT.shape
    nk = S // tk
    assert nk % 2 == 0 and nk >= 2
    return pl.pallas_call(
        functools.partial(_mla_attn_kernel, tk=tk),
        grid=(B, MLA_HEADS, S // tq),
        scratch_shapes=[pltpu.VMEM((tk, tq), F32), pltpu.VMEM((tk, tq), F32)],
        in_specs=[
            pl.BlockSpec((None, MLA_HEAD_PAD, tq), lambda b, h, i: (b, h, i)),
            pl.BlockSpec((None, S, MLA_HEAD_PAD), lambda b, h, i: (b, 0, h)),
            pl.BlockSpec((None, None, nk, MLA_V_DIM, tk), lambda b, h, i: (b, h, 0, 0, 0)),
        ],
        out_specs=pl.BlockSpec((None, MLA_V_DIM, tq), lambda b, h, i: (b, h, i)),
        out_shape=jax.ShapeDtypeStruct((B, MLA_WIDTH, S), BF16),
        compiler_params=pltpu.CompilerParams(
            dimension_semantics=("parallel", "parallel", "parallel"),
            vmem_limit_bytes=VMEM_LIMIT_BYTES),
        name="mla_attn",
    )(qT, k, vT)


def _dil_attn_kernel(q_ref, k_ref, v_ref, o_ref, lse_ref, *, tl, side):
    L = k_ref.shape[0]
    win = tl + 2 * side
    n = pl.program_id(2)
    q0 = n * tl
    ks = pl.multiple_of(jnp.clip(q0 - side, 0, L - win), side)
    a = lax.broadcasted_iota(jnp.int32, (tl, win), 0)
    c = lax.broadcasted_iota(jnp.int32, (tl, win), 1)
    valid = jnp.abs((a - c) + (q0 - ks)) <= side
    lane = lax.broadcasted_iota(jnp.int32, (1, LANES), 1)
    lo = lane < DIL_HEAD_DIM
    scale = DIL_HEAD_DIM ** -0.5
    for pair in range(DIL_HEADS_PER_GROUP // 2):
        sl = slice(pair * LANES, (pair + 1) * LANES)
        qp = q_ref[:, sl]
        kp = k_ref[pl.ds(ks, win), sl]
        vp = v_ref[pl.ds(ks, win), sl]
        outs, lses = [], []
        for half in (lo, ~lo):
            qh = jnp.where(half, qp, jnp.zeros_like(qp))
            s = lax.dot_general(qh, kp, (((1,), (1,)), ((), ())),
                                preferred_element_type=F32) * scale
            s = jnp.where(valid, s, NEG_INF)
            m = jnp.max(s, axis=-1, keepdims=True)
            p = jnp.exp(s - m)
            den = jnp.sum(p, axis=-1, keepdims=True)
            outs.append(_dot(p.astype(BF16), vp) / den)
            lses.append(m + jnp.log(den))
        o_ref[:, sl] = jnp.where(lo, outs[0], outs[1]).astype(o_ref.dtype)
        lse_ref[:, sl] = jnp.where(lo, lses[0], lses[1])


def _dilated_group(qd, kd, vd, g, dilation, side, *, tl):
    B, S, _ = qd.shape
    L = S // dilation
    tl = min(tl, L - 2 * side)
    view = lambda a: a.reshape(B, L, dilation * DIL_QKV_WIDTH)
    col = lambda r: r * DIL_GROUPS + g
    q_spec = pl.BlockSpec((None, tl, DIL_WIDTH), lambda b, r, n: (b, n, col(r)))
    kv_spec = pl.BlockSpec((None, L, DIL_WIDTH), lambda b, r, n: (b, 0, col(r)),
                           pipeline_mode=pl.Buffered(1))
    out_spec = pl.BlockSpec((None, tl, DIL_WIDTH), lambda b, r, n: (b, n, r))
    o, lse = pl.pallas_call(
        functools.partial(_dil_attn_kernel, tl=tl, side=side),
        grid=(B, dilation, L // tl),
        in_specs=[q_spec, kv_spec, kv_spec],
        out_specs=[out_spec, out_spec],
        out_shape=[jax.ShapeDtypeStruct((B, L, dilation * DIL_WIDTH), BF16),
                   jax.ShapeDtypeStruct((B, L, dilation * DIL_WIDTH), F32)],
        compiler_params=pltpu.CompilerParams(
            dimension_semantics=("parallel", "parallel", "arbitrary"),
            vmem_limit_bytes=VMEM_LIMIT_BYTES),
        name=f"dil_attn_g{g}",
    )(view(qd), view(kd), view(vd))
    return o.reshape(B * S, DIL_WIDTH), lse.reshape(B * S, DIL_WIDTH)


def _out_kernel(x_ref, a_ref, o0_ref, o1_ref, o2_ref, l0_ref, l1_ref, l2_ref,
                gattn_ref, wzg_ref, bg_ref, womla_ref, wodil_ref, wout_ref, gfin_ref, out_ref,
                *, final_norm):
    x = x_ref[...]
    hb = _rms(x, gattn_ref[...]).astype(BF16)
    zg = _dot(hb, wzg_ref[...])
    z_mla = zg[:, :MLA_WIDTH]
    z_dil = zg[:, MLA_WIDTH:MLA_WIDTH + DIL_WIDTH]
    g = zg[:, MLA_WIDTH + DIL_WIDTH:] + bg_ref[...]
    g_mla, g_dil = g[:, :D_MODEL], g[:, D_MODEL:]

    y_mla = _dot((a_ref[...].astype(F32) * jax.nn.silu(z_mla)).astype(BF16), womla_ref[...])

    l0, l1, l2 = l0_ref[...], l1_ref[...], l2_ref[...]
    mx = jnp.maximum(jnp.maximum(l0, l1), l2)
    e0, e1, e2 = jnp.exp(l0 - mx), jnp.exp(l1 - mx), jnp.exp(l2 - mx)
    den = e0 + e1 + e2
    d = (e0 / den * o0_ref[...].astype(F32) + e1 / den * o1_ref[...].astype(F32)
         + e2 / den * o2_ref[...].astype(F32))
    y_dil = _dot((d * jax.nn.silu(z_dil)).astype(BF16), wodil_ref[...])

    merged = jax.nn.sigmoid(g_mla) * y_mla + jax.nn.sigmoid(g_dil) * y_dil
    y = x + _dot(merged.astype(BF16), wout_ref[...])
    out_ref[...] = _rms(y, gfin_ref[...]) if final_norm else y


def _output(x2, a_out, os, lses, attn_norm_g, w_in, b_gate, w_o_mla, w_o_dil, w_out, final_norm_g, *,
            tm, final_norm):
    T = x2.shape[0]
    o = [0]
    for s in IN_SPLITS:
        o.append(o[-1] + s)
    w_zg = jnp.concatenate([w_in[:, o[3]:o[4]], w_in[:, o[7]:o[10]]], axis=1).astype(BF16)
    full = lambda a: pl.BlockSpec(a.shape, lambda i: (0,) * a.ndim)
    rows = lambda w: pl.BlockSpec((tm, w), lambda i: (i, 0))
    params = [attn_norm_g.reshape(1, -1), w_zg, b_gate.reshape(1, -1), w_o_mla.astype(BF16),
              w_o_dil.astype(BF16), w_out.astype(BF16), final_norm_g.reshape(1, -1)]
    return pl.pallas_call(
        functools.partial(_out_kernel, final_norm=final_norm),
        grid=(T // tm,),
        in_specs=[rows(D_MODEL)] + [rows(DIL_WIDTH)] * 7 + [full(a) for a in params],
        out_specs=rows(D_MODEL),
        out_shape=jax.ShapeDtypeStruct((T, D_MODEL), F32),
        compiler_params=pltpu.CompilerParams(dimension_semantics=("parallel",),
                                             vmem_limit_bytes=VMEM_LIMIT_BYTES),
        name="out",
    )(x2, a_out, *os, *lses, *params)


def kernel(x, positions, attn_norm_g, w_in, b_gate, mla_q_norm_g, mla_kv_norm_g, w_uq, w_ukv,
           w_o_mla, w_o_dil, w_out, final_norm_g):
    B, S, _ = x.shape
    depth = w_in.shape[0]
    h = x.reshape(B * S, D_MODEL)
    pos2 = positions.reshape(B * S, 1)
    tk = 1024
    for layer in range(depth):
        q, k, v, qd, kd, vd = _projection(
            h, pos2, attn_norm_g[layer], w_in[layer], mla_q_norm_g[layer], mla_kv_norm_g[layer],
            w_uq[layer], w_ukv[layer], tm=256)
        qk_w = MLA_HEADS * MLA_HEAD_PAD
        qT = q.reshape(B, S, qk_w).transpose(0, 2, 1)
        vT = v.reshape(B, S // tk, tk, MLA_HEADS, MLA_V_DIM).transpose(0, 3, 1, 4, 2)
        oT = _mla_attention(qT, k.reshape(B, S, qk_w), vT, tq=512, tk=tk)
        a_out = oT.transpose(0, 2, 1).reshape(B * S, MLA_WIDTH)

        shp = (B, S, DIL_QKV_WIDTH)
        os, lses = [], []
        for g, (window, dilation) in enumerate(DIL_CONFIGS):
            o_g, lse_g = _dilated_group(qd.reshape(shp), kd.reshape(shp), vd.reshape(shp),
                                        g, dilation, window // (2 * dilation), tl=256)
            os.append(o_g)
            lses.append(lse_g)
        h = _output(h, a_out, os, lses, attn_norm_g[layer], w_in[layer], b_gate[layer],
                    w_o_mla[layer], w_o_dil[layer], w_out[layer], final_norm_g, tm=256,
                    final_norm=(layer == depth - 1))
    return h.reshape(B, S, D_MODEL)
```

```python
import functools

import jax
import jax.numpy as jnp
from jax import lax
from jax.experimental import pallas as pl
from jax.experimental.pallas import tpu as pltpu

D_MODEL = 1024
ROPE_THETA = 10000.0
RMS_EPS = 1e-6
NEG_INF = -1e30

MLA_HEADS = 8
MLA_Q_RANK = 256
MLA_KV_RANK = 128
MLA_NOPE_DIM = 64
MLA_ROPE_DIM = 32
MLA_V_DIM = 64
MLA_QK_DIM = MLA_NOPE_DIM + MLA_ROPE_DIM
MLA_WIDTH = MLA_HEADS * MLA_V_DIM

DIL_CONFIGS = ((128, 1), (512, 4), (2048, 16))
DIL_GROUPS = len(DIL_CONFIGS)
DIL_HEADS_PER_GROUP = 8
DIL_HEAD_DIM = 64
DIL_QKV_WIDTH = DIL_GROUPS * DIL_HEADS_PER_GROUP * DIL_HEAD_DIM
DIL_WIDTH = DIL_HEADS_PER_GROUP * DIL_HEAD_DIM

IN_SPLITS = (MLA_Q_RANK, MLA_KV_RANK, MLA_ROPE_DIM, MLA_WIDTH,
             DIL_QKV_WIDTH, DIL_QKV_WIDTH, DIL_QKV_WIDTH, DIL_WIDTH,
             D_MODEL, D_MODEL)

LANES = 128
BF16_ROWS = 16
MLA_HEAD_PAD = LANES
VMEM_LIMIT_BYTES = 56 * 1024 * 1024

TOKEN_TILE = 256
MLA_Q_TILE = 512
MLA_K_TILE = 1024
DIL_Q_TILE = 256

BF16 = jnp.bfloat16
F32 = jnp.float32
LOG2_E = 1.4426950408889634
LN_2 = 0.6931471805599453

_NT = (((1,), (1,)), ((), ()))
_TN = (((0,), (0,)), ((), ()))


def _dot(a, b):
    return jnp.dot(a, b, preferred_element_type=F32)


def _rms(x, g):
    return x * lax.rsqrt(jnp.mean(x * x, axis=-1, keepdims=True) + RMS_EPS) * g


def _rope_tile(x, cos, sin_signed, first_half, half):
    rot = jnp.where(first_half, pltpu.roll(x, LANES - half, axis=1), pltpu.roll(x, half, axis=1))
    return x * cos + rot * sin_signed


def _split_offsets():
    o = [0]
    for s in IN_SPLITS:
        o.append(o[-1] + s)
    return o


def _residue_perm(tm, d):
    rows = jnp.arange(tm)
    src = (rows % (tm // d)) * d + rows // (tm // d)
    return (src[:, None] == jnp.arange(tm)[None, :]).astype(BF16)


def _proj_kernel(x_ref, pos_ref, gattn_ref, wmla_ref, wdil_ref, gq_ref, gkv_ref,
                 wuq_ref, wuk_ref, wuv_ref, tab_ref, *rest):
    perm_refs = rest[:DIL_GROUPS - 1]
    q_ref, k_ref, v_ref = rest[DIL_GROUPS - 1:DIL_GROUPS + 2]
    dil_refs = rest[DIL_GROUPS + 2:]
    x = x_ref[...]
    hb = _rms(x, gattn_ref[...]).astype(BF16)
    pos = pos_ref[...].astype(F32)

    tab = tab_ref[...]
    lane = lax.broadcasted_iota(jnp.int32, (1, LANES), 1)

    ang_m = pos * tab[0:1, :]
    cos_m, sin_m = jnp.cos(ang_m), jnp.sin(ang_m) * tab[1:2, :]
    first_m = lane < (MLA_NOPE_DIM + MLA_ROPE_DIM // 2)
    q_scale = MLA_QK_DIM ** -0.5 * LOG2_E

    mla = _dot(hb, wmla_ref[...])
    c_q = mla[:, :MLA_Q_RANK]
    c_kv = mla[:, MLA_Q_RANK:MLA_Q_RANK + MLA_KV_RANK]
    kpe = mla[:, MLA_Q_RANK + MLA_KV_RANK:]
    kpe = _rope_tile(kpe, cos_m, sin_m, first_m, MLA_ROPE_DIM // 2)

    cqn = _rms(c_q, gq_ref[...]).astype(BF16)
    ckn = _rms(c_kv, gkv_ref[...]).astype(BF16)
    q = _dot(cqn, wuq_ref[...])
    kn = _dot(ckn, wuk_ref[...])
    v_ref[...] = _dot(ckn, wuv_ref[...]).astype(BF16)
    cos_q, sin_q = cos_m * q_scale, sin_m * q_scale
    for h in range(MLA_HEADS):
        sl = slice(h * MLA_HEAD_PAD, (h + 1) * MLA_HEAD_PAD)
        q_ref[:, sl] = _rope_tile(q[:, sl], cos_q, sin_q, first_m, MLA_ROPE_DIM // 2).astype(BF16)
        k_ref[:, sl] = (kn[:, sl] + kpe).astype(BF16)

    ang_d = pos * tab[2:3, :]
    cos_d, sin_d = jnp.cos(ang_d), jnp.sin(ang_d) * tab[3:4, :]
    first_d = (lane % DIL_HEAD_DIM) < (DIL_HEAD_DIM // 2)
    qd_scale = DIL_HEAD_DIM ** -0.5 * LOG2_E
    tiles_per_group = DIL_WIDTH // LANES
    for part in range(3):
        y = _dot(hb, wdil_ref[:, part * DIL_QKV_WIDTH:(part + 1) * DIL_QKV_WIDTH])
        cos_p, sin_p = (cos_d * qd_scale, sin_d * qd_scale) if part == 0 else (cos_d, sin_d)
        for g in range(DIL_GROUPS):
            cols = []
            for t in range(g * tiles_per_group, (g + 1) * tiles_per_group):
                yt = y[:, t * LANES:(t + 1) * LANES]
                if part < 2:
                    yt = _rope_tile(yt, cos_p, sin_p, first_d, DIL_HEAD_DIM // 2)
                cols.append(yt.astype(BF16))
            yg = jnp.concatenate(cols, axis=1)
            out_ref = dil_refs[part * DIL_GROUPS + g]
            d = out_ref.shape[0]
            if d > 1:
                yg = _dot(perm_refs[g - 1][...], yg).astype(BF16)
            out_ref[...] = yg.reshape(out_ref.shape)


def _rope_tables():
    lane = jnp.arange(LANES)
    inv_m = ROPE_THETA ** (-jnp.arange(0, MLA_ROPE_DIM, 2, dtype=F32) / MLA_ROPE_DIM)
    inv_d = ROPE_THETA ** (-jnp.arange(0, DIL_HEAD_DIM, 2, dtype=F32) / DIL_HEAD_DIM)
    in_rope = (lane >= MLA_NOPE_DIM) & (lane < MLA_QK_DIM)
    r = lane - MLA_NOPE_DIM
    tab_m = jnp.where(in_rope, inv_m[jnp.clip(r, 0, MLA_ROPE_DIM - 1) % (MLA_ROPE_DIM // 2)], 0.0)
    sgn_m = jnp.where(in_rope, jnp.where(r < MLA_ROPE_DIM // 2, -1.0, 1.0), 0.0)
    e = lane % DIL_HEAD_DIM
    tab_d = inv_d[e % (DIL_HEAD_DIM // 2)]
    sgn_d = jnp.where(e < DIL_HEAD_DIM // 2, -1.0, 1.0)
    zeros = jnp.zeros((LANES,), F32)
    return jnp.stack([tab_m, sgn_m, tab_d, sgn_d, zeros, zeros, zeros, zeros]).astype(F32)


def _projection(x3, pos3, attn_norm_g, w_in, gq, gkv, w_uq, w_ukv):
    B, S, _ = x3.shape
    tm = TOKEN_TILE
    o = _split_offsets()
    w_cq, w_ckv, w_kr = w_in[:, o[0]:o[1]], w_in[:, o[1]:o[2]], w_in[:, o[2]:o[3]]
    kpe_slot = jnp.zeros((D_MODEL, LANES), F32).at[:, MLA_NOPE_DIM:MLA_QK_DIM].set(w_kr)
    w_mla = jnp.concatenate([w_cq, w_ckv, kpe_slot], axis=1).astype(BF16)
    w_dil = w_in[:, o[4]:o[7]].astype(BF16)
    wq = w_uq.reshape(MLA_Q_RANK, MLA_HEADS, MLA_QK_DIM)
    wq = jnp.pad(wq, ((0, 0), (0, 0), (0, MLA_HEAD_PAD - MLA_QK_DIM)))
    wq = wq.reshape(MLA_Q_RANK, MLA_HEADS * MLA_HEAD_PAD).astype(BF16)
    wkv = w_ukv.reshape(MLA_KV_RANK, MLA_HEADS, MLA_NOPE_DIM + MLA_V_DIM)
    wk = jnp.pad(wkv[:, :, :MLA_NOPE_DIM], ((0, 0), (0, 0), (0, MLA_HEAD_PAD - MLA_NOPE_DIM)))
    wk = wk.reshape(MLA_KV_RANK, MLA_HEADS * MLA_HEAD_PAD).astype(BF16)
    wv = wkv[:, :, MLA_NOPE_DIM:].reshape(MLA_KV_RANK, MLA_WIDTH).astype(BF16)
    perms = [_residue_perm(tm, d) for _, d in DIL_CONFIGS[1:]]

    full = lambda a: pl.BlockSpec(a.shape, lambda b, i: (0,) * a.ndim)
    rows = lambda w: pl.BlockSpec((None, tm, w), lambda b, i: (b, i, 0))
    params = [attn_norm_g.reshape(1, -1), w_mla, w_dil, gq.reshape(1, -1), gkv.reshape(1, -1),
              wq, wk, wv, _rope_tables()] + perms
    qk_w = MLA_HEADS * MLA_HEAD_PAD
    mla_widths = (qk_w, qk_w, MLA_WIDTH)
    out_specs = [rows(w) for w in mla_widths]
    out_shape = [jax.ShapeDtypeStruct((B, S, w), BF16) for w in mla_widths]
    for _ in range(3):
        for _, d in DIL_CONFIGS:
            out_specs.append(pl.BlockSpec((None, d, tm // d, DIL_WIDTH), lambda b, i: (b, 0, i, 0)))
            out_shape.append(jax.ShapeDtypeStruct((B, d, S // d, DIL_WIDTH), BF16))
    outs = pl.pallas_call(
        _proj_kernel,
        grid=(B, S // tm),
        in_specs=[rows(D_MODEL), rows(1)] + [full(a) for a in params],
        out_specs=out_specs,
        out_shape=out_shape,
        compiler_params=pltpu.CompilerParams(dimension_semantics=("parallel", "parallel"),
                                             vmem_limit_bytes=VMEM_LIMIT_BYTES),
        name="proj",
    )(x3, pos3, *params)
    return outs[:3], [outs[3 + p * DIL_GROUPS:3 + (p + 1) * DIL_GROUPS] for p in range(3)]


def _mla_attn_kernel(qT_ref, k_ref, vT_ref, oT_ref, s0_ref, s1_ref, *, tk):
    qT = qT_ref[...]
    tq = qT.shape[1]
    nk = k_ref.shape[0] // tk

    def scores(j):
        kc = k_ref[pl.ds(pl.multiple_of(j * tk, tk), tk), :]
        return _dot(kc, qT)

    def update(s_ref, j, carry):
        m, l, acc = carry
        s = s_ref[...]
        m_new = jnp.maximum(m, jnp.max(s, axis=0, keepdims=True))
        alpha = jnp.exp2(m - m_new)
        p = jnp.exp2(s - m_new)
        l = alpha * l + jnp.sum(p, axis=0, keepdims=True)
        acc = alpha * acc + _dot(vT_ref[j], p.astype(BF16))
        return m_new, l, acc

    carry = (jnp.full((1, tq), -jnp.inf, F32), jnp.zeros((1, tq), F32),
             jnp.zeros((MLA_V_DIM, tq), F32))
    bufs = (s0_ref, s1_ref)
    s0_ref[...] = scores(0)
    for j in range(nk):
        if j + 1 < nk:
            bufs[(j + 1) % 2][...] = scores(j + 1)
        carry = update(bufs[j % 2], j, carry)
    _, l, acc = carry
    oT_ref[...] = (acc / l).astype(oT_ref.dtype)


def _mla_attention(qT, k, vT):
    B, _, S = qT.shape
    tq, tk = MLA_Q_TILE, MLA_K_TILE
    nk = S // tk
    return pl.pallas_call(
        functools.partial(_mla_attn_kernel, tk=tk),
        grid=(B, MLA_HEADS, S // tq),
        scratch_shapes=[pltpu.VMEM((tk, tq), F32), pltpu.VMEM((tk, tq), F32)],
        in_specs=[
            pl.BlockSpec((None, MLA_HEAD_PAD, tq), lambda b, h, i: (b, h, i)),
            pl.BlockSpec((None, S, MLA_HEAD_PAD), lambda b, h, i: (b, 0, h)),
            pl.BlockSpec((None, None, nk, MLA_V_DIM, tk), lambda b, h, i: (b, h, 0, 0, 0)),
        ],
        out_specs=pl.BlockSpec((None, MLA_V_DIM, tq), lambda b, h, i: (b, h, i)),
        out_shape=jax.ShapeDtypeStruct((B, MLA_WIDTH, S), BF16),
        compiler_params=pltpu.CompilerParams(
            dimension_semantics=("parallel", "parallel", "parallel"),
            vmem_limit_bytes=VMEM_LIMIT_BYTES),
        name="mla_attn",
    )(qT, k, vT)


def _dil_attn_kernel(q_ref, k_ref, v_ref, o_ref, lse_ref, s0_ref, s1_ref, *, tl, side):
    L = k_ref.shape[0]
    win = tl + 2 * side
    q0 = pl.program_id(2) * tl
    ks = pl.multiple_of(jnp.clip(q0 - side, 0, L - win), side)
    c = lax.broadcasted_iota(jnp.int32, (win, tl), 0)
    a = lax.broadcasted_iota(jnp.int32, (win, tl), 1)
    bias = jnp.where(jnp.abs((a - c) + (q0 - ks)) <= side, 0.0, NEG_INF)
    lane = lax.broadcasted_iota(jnp.int32, (1, LANES), 1)
    lo = lane < DIL_HEAD_DIM
    n_pairs = DIL_HEADS_PER_GROUP // 2
    lanes_of = lambda pair: slice(pair * LANES, (pair + 1) * LANES)

    def scores(h):
        pair, hh = divmod(h, 2)
        qp = q_ref[:, lanes_of(pair)]
        kp = k_ref[pl.ds(ks, win), lanes_of(pair)]
        qh = jnp.where(lo if hh == 0 else ~lo, qp, jnp.zeros_like(qp))
        return lax.dot_general(kp, qh, _NT, preferred_element_type=F32) + bias

    bufs = (s0_ref, s1_ref)
    s0_ref[...] = scores(0)
    for pair in range(n_pairs):
        sl = lanes_of(pair)
        vp = v_ref[pl.ds(ks, win), sl]
        o_rows, lse_rows = [], []
        for hh in range(2):
            h = 2 * pair + hh
            if h + 1 < DIL_HEADS_PER_GROUP:
                bufs[(h + 1) % 2][...] = scores(h + 1)
            s = bufs[h % 2][...]
            m = jnp.max(s, axis=0, keepdims=True)
            p = jnp.exp2(s - m)
            den = jnp.sum(p, axis=0, keepdims=True)
            oT = lax.dot_general(vp, p.astype(BF16), _TN, preferred_element_type=F32)
            o_rows.append(oT[hh * DIL_HEAD_DIM:(hh + 1) * DIL_HEAD_DIM] / den)
            lse_rows.append(jnp.broadcast_to(m * LN_2 + jnp.log(den), (DIL_HEAD_DIM, tl)))
        o_ref[:, sl] = jnp.concatenate(o_rows, axis=0).T.astype(o_ref.dtype)
        lse_ref[:, sl] = jnp.concatenate(lse_rows, axis=0).T


def _dilated_group(q, k, v, side):
    B, d, L, _ = q.shape
    tl = min(DIL_Q_TILE, L - 2 * side)
    q_spec = pl.BlockSpec((None, None, tl, DIL_WIDTH), lambda b, r, n: (b, r, n, 0))
    kv_spec = pl.BlockSpec((None, None, L, DIL_WIDTH), lambda b, r, n: (b, r, 0, 0),
                           pipeline_mode=pl.Buffered(1))
    return pl.pallas_call(
        functools.partial(_dil_attn_kernel, tl=tl, side=side),
        grid=(B, d, L // tl),
        scratch_shapes=[pltpu.VMEM((tl + 2 * side, tl), F32)] * 2,
        in_specs=[q_spec, kv_spec, kv_spec],
        out_specs=[q_spec, q_spec],
        out_shape=[jax.ShapeDtypeStruct(q.shape, BF16), jax.ShapeDtypeStruct(q.shape, F32)],
        compiler_params=pltpu.CompilerParams(
            dimension_semantics=("parallel", "parallel", "arbitrary"),
            vmem_limit_bytes=VMEM_LIMIT_BYTES),
        name=f"dil_attn_d{d}",
    )(q, k, v)


def _exact_gather_f32(perm, x):
    hi = x.astype(BF16)
    r1 = x - hi.astype(F32)
    mid = r1.astype(BF16)
    lo = (r1 - mid.astype(F32)).astype(BF16)
    return _dot(perm, hi) + _dot(perm, mid) + _dot(perm, lo)


def _out_kernel(x_ref, a_ref, o0_ref, o1_ref, o2_ref, l0_ref, l1_ref, l2_ref, p1_ref, p2_ref,
                gattn_ref, wzg_ref, bg_ref, womla_ref, wodil_ref, wout_ref, gfin_ref, out_ref,
                *, final_norm):
    x = x_ref[...]
    tm = x.shape[0]
    hb = _rms(x, gattn_ref[...]).astype(BF16)
    zg = _dot(hb, wzg_ref[...])
    z_mla = zg[:, :MLA_WIDTH]
    z_dil = zg[:, MLA_WIDTH:MLA_WIDTH + DIL_WIDTH]
    g = zg[:, MLA_WIDTH + DIL_WIDTH:] + bg_ref[...]
    g_mla, g_dil = g[:, :D_MODEL], g[:, D_MODEL:]

    y_mla = _dot((a_ref[...].astype(F32) * jax.nn.silu(z_mla)).astype(BF16), womla_ref[...])

    o0 = o0_ref[...].reshape(tm, DIL_WIDTH).astype(F32)
    l0 = l0_ref[...].reshape(tm, DIL_WIDTH)
    o1 = _dot(p1_ref[...], o1_ref[...].reshape(tm, DIL_WIDTH))
    l1 = _exact_gather_f32(p1_ref[...], l1_ref[...].reshape(tm, DIL_WIDTH))
    o2 = _dot(p2_ref[...], o2_ref[...].reshape(tm, DIL_WIDTH))
    l2 = _exact_gather_f32(p2_ref[...], l2_ref[...].reshape(tm, DIL_WIDTH))
    mx = jnp.maximum(jnp.maximum(l0, l1), l2)
    e0, e1, e2 = jnp.exp(l0 - mx), jnp.exp(l1 - mx), jnp.exp(l2 - mx)
    den = e0 + e1 + e2
    d = e0 / den * o0 + e1 / den * o1 + e2 / den * o2
    y_dil = _dot((d * jax.nn.silu(z_dil)).astype(BF16), wodil_ref[...])

    merged = jax.nn.sigmoid(g_mla) * y_mla + jax.nn.sigmoid(g_dil) * y_dil
    y = x + _dot(merged.astype(BF16), wout_ref[...])
    out_ref[...] = _rms(y, gfin_ref[...]) if final_norm else y


def _output(x3, a_out, os, lses, attn_norm_g, w_in, b_gate, w_o_mla, w_o_dil, w_out, final_norm_g, *,
            final_norm):
    B, S, _ = x3.shape
    tm = TOKEN_TILE
    o = _split_offsets()
    w_zg = jnp.concatenate([w_in[:, o[3]:o[4]], w_in[:, o[7]:o[10]]], axis=1).astype(BF16)
    inv_perms = [_residue_perm(tm, d).T for _, d in DIL_CONFIGS[1:]]
    full = lambda a: pl.BlockSpec(a.shape, lambda b, i: (0,) * a.ndim)
    rows = lambda w: pl.BlockSpec((None, tm, w), lambda b, i: (b, i, 0))
    group = lambda d: pl.BlockSpec((None, d, tm // d, DIL_WIDTH), lambda b, i: (b, 0, i, 0))
    groups = [group(d) for _, d in DIL_CONFIGS]
    params = inv_perms + [attn_norm_g.reshape(1, -1), w_zg, b_gate.reshape(1, -1), w_o_mla.astype(BF16),
                          w_o_dil.astype(BF16), w_out.astype(BF16), final_norm_g.reshape(1, -1)]
    return pl.pallas_call(
        functools.partial(_out_kernel, final_norm=final_norm),
        grid=(B, S // tm),
        in_specs=[rows(D_MODEL), rows(MLA_WIDTH)] + groups + groups + [full(a) for a in params],
        out_specs=rows(D_MODEL),
        out_shape=jax.ShapeDtypeStruct((B, S, D_MODEL), F32),
        compiler_params=pltpu.CompilerParams(dimension_semantics=("parallel", "parallel"),
                                             vmem_limit_bytes=VMEM_LIMIT_BYTES),
        name="out",
    )(x3, a_out, *os, *lses, *params)


def kernel(x, positions, attn_norm_g, w_in, b_gate, mla_q_norm_g, mla_kv_norm_g, w_uq, w_ukv,
           w_o_mla, w_o_dil, w_out, final_norm_g):
    B, S, _ = x.shape
    depth = w_in.shape[0]
    h = x
    pos3 = positions.reshape(B, S, 1)
    tk = MLA_K_TILE
    for layer in range(depth):
        (q, k, v), (qd, kd, vd) = _projection(
            h, pos3, attn_norm_g[layer], w_in[layer], mla_q_norm_g[layer], mla_kv_norm_g[layer],
            w_uq[layer], w_ukv[layer])
        qT = q.transpose(0, 2, 1)
        vT = v.reshape(B, S // tk, tk, MLA_HEADS, MLA_V_DIM).transpose(0, 3, 1, 4, 2)
        a_out = _mla_attention(qT, k, vT).transpose(0, 2, 1)

        os, lses = [], []
        for g, (window, dilation) in enumerate(DIL_CONFIGS):
            o_g, lse_g = _dilated_group(qd[g], kd[g], vd[g], window // (2 * dilation))
            os.append(o_g)
            lses.append(lse_g)
        h = _output(h, a_out, os, lses, attn_norm_g[layer], w_in[layer], b_gate[layer],
                    w_o_mla[layer], w_o_dil[layer], w_out[layer], final_norm_g,
                    final_norm=(layer == depth - 1))
    return h
```

```python
import functools

import jax
import jax.numpy as jnp
from jax import lax
from jax.experimental import pallas as pl
from jax.experimental.pallas import tpu as pltpu

D_MODEL = 1024
ROPE_THETA = 10000.0
RMS_EPS = 1e-6
NEG_INF = -1e30

MLA_HEADS = 8
MLA_Q_RANK = 256
MLA_KV_RANK = 128
MLA_NOPE_DIM = 64
MLA_ROPE_DIM = 32
MLA_V_DIM = 64
MLA_QK_DIM = MLA_NOPE_DIM + MLA_ROPE_DIM
MLA_WIDTH = MLA_HEADS * MLA_V_DIM

DIL_CONFIGS = ((128, 1), (512, 4), (2048, 16))
DIL_GROUPS = len(DIL_CONFIGS)
DIL_HEADS_PER_GROUP = 8
DIL_HEAD_DIM = 64
DIL_QKV_WIDTH = DIL_GROUPS * DIL_HEADS_PER_GROUP * DIL_HEAD_DIM
DIL_WIDTH = DIL_HEADS_PER_GROUP * DIL_HEAD_DIM

IN_SPLITS = (MLA_Q_RANK, MLA_KV_RANK, MLA_ROPE_DIM, MLA_WIDTH,
             DIL_QKV_WIDTH, DIL_QKV_WIDTH, DIL_QKV_WIDTH, DIL_WIDTH,
             D_MODEL, D_MODEL)

LANES = 128
BF16_ROWS = 16
MLA_HEAD_PAD = LANES
VMEM_LIMIT_BYTES = 56 * 1024 * 1024

TOKEN_TILE = 512
PERM_BLOCK = 256
MLA_Q_TILE = 512
MLA_K_TILE = 1024
DIL_Q_TILE = 256

BF16 = jnp.bfloat16
F32 = jnp.float32
LOG2_E = 1.4426950408889634
LN_2 = 0.6931471805599453

_NT = (((1,), (1,)), ((), ()))
_TN = (((0,), (0,)), ((), ()))


def _dot(a, b):
    return jnp.dot(a, b, preferred_element_type=F32)


def _rms(x, g):
    return x * lax.rsqrt(jnp.mean(x * x, axis=-1, keepdims=True) + RMS_EPS) * g


def _rope_tile(x, cos, sin_signed, first_half, half):
    rot = jnp.where(first_half, pltpu.roll(x, LANES - half, axis=1), pltpu.roll(x, half, axis=1))
    return x * cos + rot * sin_signed


def _split_offsets():
    o = [0]
    for s in IN_SPLITS:
        o.append(o[-1] + s)
    return o


def _residue_perm(tm, d):
    rows = jnp.arange(tm)
    src = (rows % (tm // d)) * d + rows // (tm // d)
    return (src[:, None] == jnp.arange(tm)[None, :]).astype(BF16)


def _proj_kernel(x_ref, pos_ref, gattn_ref, wmla_ref, wdil_ref, gq_ref, gkv_ref,
                 wuq_ref, wuk_ref, wuvT_ref, tab_ref, *rest):
    perm_refs = rest[:DIL_GROUPS - 1]
    qT_ref, k_ref, vT_ref = rest[DIL_GROUPS - 1:DIL_GROUPS + 2]
    dil_refs = rest[DIL_GROUPS + 2:]
    x = x_ref[...]
    hb = _rms(x, gattn_ref[...]).astype(BF16)
    pos = pos_ref[...].astype(F32)

    tab = tab_ref[...]
    lane = lax.broadcasted_iota(jnp.int32, (1, LANES), 1)

    ang_m = pos * tab[0:1, :]
    cos_m, sin_m = jnp.cos(ang_m), jnp.sin(ang_m) * tab[1:2, :]
    first_m = lane < (MLA_NOPE_DIM + MLA_ROPE_DIM // 2)
    q_scale = MLA_QK_DIM ** -0.5 * LOG2_E

    mla = _dot(hb, wmla_ref[...])
    c_q = mla[:, :MLA_Q_RANK]
    c_kv = mla[:, MLA_Q_RANK:MLA_Q_RANK + MLA_KV_RANK]
    kpe = mla[:, MLA_Q_RANK + MLA_KV_RANK:]
    kpe = _rope_tile(kpe, cos_m, sin_m, first_m, MLA_ROPE_DIM // 2)

    cqn = _rms(c_q, gq_ref[...]).astype(BF16)
    ckn = _rms(c_kv, gkv_ref[...]).astype(BF16)
    q = _dot(cqn, wuq_ref[...])
    kn = _dot(ckn, wuk_ref[...])
    vT_ref[...] = lax.dot_general(wuvT_ref[...], ckn, _NT,
                                  preferred_element_type=F32).astype(BF16)
    cos_q, sin_q = cos_m * q_scale, sin_m * q_scale
    for h in range(MLA_HEADS):
        sl = slice(h * MLA_HEAD_PAD, (h + 1) * MLA_HEAD_PAD)
        qh = _rope_tile(q[:, sl], cos_q, sin_q, first_m, MLA_ROPE_DIM // 2)
        qT_ref[sl, :] = qh.T.astype(BF16)
        k_ref[:, sl] = (kn[:, sl] + kpe).astype(BF16)

    ang_d = pos * tab[2:3, :]
    cos_d, sin_d = jnp.cos(ang_d), jnp.sin(ang_d) * tab[3:4, :]
    first_d = (lane % DIL_HEAD_DIM) < (DIL_HEAD_DIM // 2)
    qd_scale = DIL_HEAD_DIM ** -0.5 * LOG2_E
    tiles_per_group = DIL_WIDTH // LANES
    for part in range(3):
        y = _dot(hb, wdil_ref[:, part * DIL_QKV_WIDTH:(part + 1) * DIL_QKV_WIDTH])
        cos_p, sin_p = (cos_d * qd_scale, sin_d * qd_scale) if part == 0 else (cos_d, sin_d)
        for g in range(DIL_GROUPS):
            cols = []
            for t in range(g * tiles_per_group, (g + 1) * tiles_per_group):
                yt = y[:, t * LANES:(t + 1) * LANES]
                if part < 2:
                    yt = _rope_tile(yt, cos_p, sin_p, first_d, DIL_HEAD_DIM // 2)
                cols.append(yt.astype(BF16))
            yg = jnp.concatenate(cols, axis=1)
            out_ref = dil_refs[part * DIL_GROUPS + g]
            d = out_ref.shape[0]
            if d == 1:
                out_ref[...] = yg.reshape(out_ref.shape)
                continue
            per = PERM_BLOCK // d
            for blk in range(yg.shape[0] // PERM_BLOCK):
                rows = yg[blk * PERM_BLOCK:(blk + 1) * PERM_BLOCK]
                gathered = _dot(perm_refs[g - 1][...], rows).astype(BF16)
                out_ref[:, blk * per:(blk + 1) * per, :] = gathered.reshape(d, per, DIL_WIDTH)


def _rope_tables():
    lane = jnp.arange(LANES)
    inv_m = ROPE_THETA ** (-jnp.arange(0, MLA_ROPE_DIM, 2, dtype=F32) / MLA_ROPE_DIM)
    inv_d = ROPE_THETA ** (-jnp.arange(0, DIL_HEAD_DIM, 2, dtype=F32) / DIL_HEAD_DIM)
    in_rope = (lane >= MLA_NOPE_DIM) & (lane < MLA_QK_DIM)
    r = lane - MLA_NOPE_DIM
    tab_m = jnp.where(in_rope, inv_m[jnp.clip(r, 0, MLA_ROPE_DIM - 1) % (MLA_ROPE_DIM // 2)], 0.0)
    sgn_m = jnp.where(in_rope, jnp.where(r < MLA_ROPE_DIM // 2, -1.0, 1.0), 0.0)
    e = lane % DIL_HEAD_DIM
    tab_d = inv_d[e % (DIL_HEAD_DIM // 2)]
    sgn_d = jnp.where(e < DIL_HEAD_DIM // 2, -1.0, 1.0)
    zeros = jnp.zeros((LANES,), F32)
    return jnp.stack([tab_m, sgn_m, tab_d, sgn_d, zeros, zeros, zeros, zeros]).astype(F32)


def _projection(x3, pos3, attn_norm_g, w_in, gq, gkv, w_uq, w_ukv):
    B, S, _ = x3.shape
    tm = TOKEN_TILE
    o = _split_offsets()
    w_cq, w_ckv, w_kr = w_in[:, o[0]:o[1]], w_in[:, o[1]:o[2]], w_in[:, o[2]:o[3]]
    kpe_slot = jnp.zeros((D_MODEL, LANES), F32).at[:, MLA_NOPE_DIM:MLA_QK_DIM].set(w_kr)
    w_mla = jnp.concatenate([w_cq, w_ckv, kpe_slot], axis=1).astype(BF16)
    w_dil = w_in[:, o[4]:o[7]].astype(BF16)
    wq = w_uq.reshape(MLA_Q_RANK, MLA_HEADS, MLA_QK_DIM)
    wq = jnp.pad(wq, ((0, 0), (0, 0), (0, MLA_HEAD_PAD - MLA_QK_DIM)))
    wq = wq.reshape(MLA_Q_RANK, MLA_HEADS * MLA_HEAD_PAD).astype(BF16)
    wkv = w_ukv.reshape(MLA_KV_RANK, MLA_HEADS, MLA_NOPE_DIM + MLA_V_DIM)
    wk = jnp.pad(wkv[:, :, :MLA_NOPE_DIM], ((0, 0), (0, 0), (0, MLA_HEAD_PAD - MLA_NOPE_DIM)))
    wk = wk.reshape(MLA_KV_RANK, MLA_HEADS * MLA_HEAD_PAD).astype(BF16)
    wv = wkv[:, :, MLA_NOPE_DIM:].reshape(MLA_KV_RANK, MLA_WIDTH).T.astype(BF16)
    perms = [_residue_perm(PERM_BLOCK, d) for _, d in DIL_CONFIGS[1:]]

    full = lambda a: pl.BlockSpec(a.shape, lambda b, i: (0,) * a.ndim, pipeline_mode=pl.Buffered(1))
    rows = lambda w: pl.BlockSpec((None, tm, w), lambda b, i: (b, i, 0))
    cols = lambda h: pl.BlockSpec((None, h, tm), lambda b, i: (b, 0, i))
    params = [attn_norm_g.reshape(1, -1), w_mla, w_dil, gq.reshape(1, -1), gkv.reshape(1, -1),
              wq, wk, wv, _rope_tables()] + perms
    qk_w = MLA_HEADS * MLA_HEAD_PAD
    out_specs = [cols(qk_w), rows(qk_w), cols(MLA_WIDTH)]
    out_shape = [jax.ShapeDtypeStruct((B, qk_w, S), BF16), jax.ShapeDtypeStruct((B, S, qk_w), BF16),
                 jax.ShapeDtypeStruct((B, MLA_WIDTH, S), BF16)]
    for _ in range(3):
        for _, d in DIL_CONFIGS:
            out_specs.append(pl.BlockSpec((None, d, tm // d, DIL_WIDTH), lambda b, i: (b, 0, i, 0)))
            out_shape.append(jax.ShapeDtypeStruct((B, d, S // d, DIL_WIDTH), BF16))
    outs = pl.pallas_call(
        _proj_kernel,
        grid=(B, S // tm),
        in_specs=[rows(D_MODEL), rows(1)] + [full(a) for a in params],
        out_specs=out_specs,
        out_shape=out_shape,
        compiler_params=pltpu.CompilerParams(dimension_semantics=("parallel", "parallel"),
                                             vmem_limit_bytes=VMEM_LIMIT_BYTES),
        name="proj",
    )(x3, pos3, *params)
    return outs[:3], [outs[3 + p * DIL_GROUPS:3 + (p + 1) * DIL_GROUPS] for p in range(3)]


def _mla_attn_kernel(qT_ref, k_ref, vT_ref, oT_ref, s0_ref, s1_ref, *, tk):
    qT = qT_ref[...]
    tq = qT.shape[1]
    nk = k_ref.shape[0] // tk

    def scores(j):
        return _dot(k_ref[j * tk:(j + 1) * tk, :], qT)

    def update(s_ref, j, carry):
        m, l, acc = carry
        s = s_ref[...]
        m_new = jnp.maximum(m, jnp.max(s, axis=0, keepdims=True))
        alpha = jnp.exp2(m - m_new)
        p = jnp.exp2(s - m_new)
        l = alpha * l + jnp.sum(p, axis=0, keepdims=True)
        acc = alpha * acc + _dot(vT_ref[:, j * tk:(j + 1) * tk], p.astype(BF16))
        return m_new, l, acc

    carry = (jnp.full((1, tq), -jnp.inf, F32), jnp.zeros((1, tq), F32),
             jnp.zeros((MLA_V_DIM, tq), F32))
    bufs = (s0_ref, s1_ref)
    s0_ref[...] = scores(0)
    for j in range(nk):
        if j + 1 < nk:
            bufs[(j + 1) % 2][...] = scores(j + 1)
        carry = update(bufs[j % 2], j, carry)
    _, l, acc = carry
    oT_ref[...] = (acc / l).astype(oT_ref.dtype)


def _mla_attention(qT, k, vT):
    B, _, S = qT.shape
    tq, tk = MLA_Q_TILE, MLA_K_TILE
    return pl.pallas_call(
        functools.partial(_mla_attn_kernel, tk=tk),
        grid=(B, MLA_HEADS, S // tq),
        scratch_shapes=[pltpu.VMEM((tk, tq), F32), pltpu.VMEM((tk, tq), F32)],
        in_specs=[
            pl.BlockSpec((None, MLA_HEAD_PAD, tq), lambda b, h, i: (b, h, i)),
            pl.BlockSpec((None, S, MLA_HEAD_PAD), lambda b, h, i: (b, 0, h)),
            pl.BlockSpec((None, MLA_V_DIM, S), lambda b, h, i: (b, h, 0)),
        ],
        out_specs=pl.BlockSpec((None, MLA_V_DIM, tq), lambda b, h, i: (b, h, i)),
        out_shape=jax.ShapeDtypeStruct((B, MLA_WIDTH, S), BF16),
        compiler_params=pltpu.CompilerParams(
            dimension_semantics=("parallel", "parallel", "parallel"),
            vmem_limit_bytes=VMEM_LIMIT_BYTES),
        name="mla_attn",
    )(qT, k, vT)


def _dil_attn_kernel(q_ref, k_ref, v_ref, o_ref, lse_ref, s0_ref, s1_ref, *, tl, side):
    L = k_ref.shape[0]
    win = tl + 2 * side
    q0 = pl.program_id(2) * tl
    ks = pl.multiple_of(jnp.clip(q0 - side, 0, L - win), side)
    c = lax.broadcasted_iota(jnp.int32, (win, tl), 0)
    a = lax.broadcasted_iota(jnp.int32, (win, tl), 1)
    bias = jnp.where(jnp.abs((a - c) + (q0 - ks)) <= side, 0.0, NEG_INF)
    lane = lax.broadcasted_iota(jnp.int32, (1, LANES), 1)
    lo = lane < DIL_HEAD_DIM
    n_pairs = DIL_HEADS_PER_GROUP // 2
    lanes_of = lambda pair: slice(pair * LANES, (pair + 1) * LANES)

    def scores(h):
        pair, hh = divmod(h, 2)
        qp = q_ref[:, lanes_of(pair)]
        kp = k_ref[pl.ds(ks, win), lanes_of(pair)]
        qh = jnp.where(lo if hh == 0 else ~lo, qp, jnp.zeros_like(qp))
        return lax.dot_general(kp, qh, _NT, preferred_element_type=F32) + bias

    bufs = (s0_ref, s1_ref)
    s0_ref[...] = scores(0)
    for pair in range(n_pairs):
        sl = lanes_of(pair)
        vp = v_ref[pl.ds(ks, win), sl]
        o_rows, lse_rows = [], []
        for hh in range(2):
            h = 2 * pair + hh
            if h + 1 < DIL_HEADS_PER_GROUP:
                bufs[(h + 1) % 2][...] = scores(h + 1)
            s = bufs[h % 2][...]
            m = jnp.max(s, axis=0, keepdims=True)
            p = jnp.exp2(s - m)
            den = jnp.sum(p, axis=0, keepdims=True)
            oT = lax.dot_general(vp, p.astype(BF16), _TN, preferred_element_type=F32)
            o_rows.append(oT[hh * DIL_HEAD_DIM:(hh + 1) * DIL_HEAD_DIM] / den)
            lse_rows.append(jnp.broadcast_to(m * LN_2 + jnp.log(den), (DIL_HEAD_DIM, tl)))
        o_ref[:, sl] = jnp.concatenate(o_rows, axis=0).T.astype(o_ref.dtype)
        lse_ref[:, sl] = jnp.concatenate(lse_rows, axis=0).T


def _dilated_group(q, k, v, side):
    B, d, L, _ = q.shape
    tl = DIL_Q_TILE
    while tl + 2 * side > L:
        tl //= 2
    q_spec = pl.BlockSpec((None, None, tl, DIL_WIDTH), lambda b, r, n: (b, r, n, 0))
    kv_spec = pl.BlockSpec((None, None, L, DIL_WIDTH), lambda b, r, n: (b, r, 0, 0),
                           pipeline_mode=pl.Buffered(1))
    return pl.pallas_call(
        functools.partial(_dil_attn_kernel, tl=tl, side=side),
        grid=(B, d, L // tl),
        scratch_shapes=[pltpu.VMEM((tl + 2 * side, tl), F32)] * 2,
        in_specs=[q_spec, kv_spec, kv_spec],
        out_specs=[q_spec, q_spec],
        out_shape=[jax.ShapeDtypeStruct(q.shape, BF16), jax.ShapeDtypeStruct(q.shape, F32)],
        compiler_params=pltpu.CompilerParams(
            dimension_semantics=("parallel", "parallel", "arbitrary"),
            vmem_limit_bytes=VMEM_LIMIT_BYTES),
        name=f"dil_attn_d{d}",
    )(q, k, v)


def _exact_gather_f32(perm, x):
    hi = x.astype(BF16)
    r1 = x - hi.astype(F32)
    mid = r1.astype(BF16)
    lo = (r1 - mid.astype(F32)).astype(BF16)
    return _dot(perm, hi) + _dot(perm, mid) + _dot(perm, lo)


def _out_kernel(x_ref, a_ref, o0_ref, o1_ref, o2_ref, l0_ref, l1_ref, l2_ref, p1_ref, p2_ref,
                gattn_ref, wzg_ref, bg_ref, womla_ref, wodil_ref, wout_ref, gfin_ref, out_ref,
                *, final_norm):
    x = x_ref[...]
    tm = x.shape[0]
    hb = _rms(x, gattn_ref[...]).astype(BF16)
    zg = _dot(hb, wzg_ref[...])
    z_mla = zg[:, :MLA_WIDTH]
    z_dil = zg[:, MLA_WIDTH:MLA_WIDTH + DIL_WIDTH]
    g = zg[:, MLA_WIDTH + DIL_WIDTH:] + bg_ref[...]
    g_mla, g_dil = g[:, :D_MODEL], g[:, D_MODEL:]

    a = a_ref[...].astype(F32).T
    y_mla = _dot((a * jax.nn.silu(z_mla)).astype(BF16), womla_ref[...])

    def token_order(ref, perm_ref, gather):
        d = ref.shape[0]
        per = PERM_BLOCK // d
        blocks = [gather(perm_ref[...], ref[:, b * per:(b + 1) * per, :].reshape(PERM_BLOCK, DIL_WIDTH))
                  for b in range(tm // PERM_BLOCK)]
        return jnp.concatenate(blocks, axis=0)

    o0 = o0_ref[...].reshape(tm, DIL_WIDTH).astype(F32)
    l0 = l0_ref[...].reshape(tm, DIL_WIDTH)
    o1 = token_order(o1_ref, p1_ref, _dot)
    l1 = token_order(l1_ref, p1_ref, _exact_gather_f32)
    o2 = token_order(o2_ref, p2_ref, _dot)
    l2 = token_order(l2_ref, p2_ref, _exact_gather_f32)
    mx = jnp.maximum(jnp.maximum(l0, l1), l2)
    e0, e1, e2 = jnp.exp(l0 - mx), jnp.exp(l1 - mx), jnp.exp(l2 - mx)
    den = e0 + e1 + e2
    d = e0 / den * o0 + e1 / den * o1 + e2 / den * o2
    y_dil = _dot((d * jax.nn.silu(z_dil)).astype(BF16), wodil_ref[...])

    merged = jax.nn.sigmoid(g_mla) * y_mla + jax.nn.sigmoid(g_dil) * y_dil
    y = x + _dot(merged.astype(BF16), wout_ref[...])
    out_ref[...] = _rms(y, gfin_ref[...]) if final_norm else y


def _output(x3, a_out, os, lses, attn_norm_g, w_in, b_gate, w_o_mla, w_o_dil, w_out, final_norm_g, *,
            final_norm):
    B, S, _ = x3.shape
    tm = TOKEN_TILE
    o = _split_offsets()
    w_zg = jnp.concatenate([w_in[:, o[3]:o[4]], w_in[:, o[7]:o[10]]], axis=1).astype(BF16)
    inv_perms = [_residue_perm(PERM_BLOCK, d).T for _, d in DIL_CONFIGS[1:]]
    full = lambda a: pl.BlockSpec(a.shape, lambda b, i: (0,) * a.ndim, pipeline_mode=pl.Buffered(1))
    rows = lambda w: pl.BlockSpec((None, tm, w), lambda b, i: (b, i, 0))
    group = lambda d: pl.BlockSpec((None, d, tm // d, DIL_WIDTH), lambda b, i: (b, 0, i, 0))
    groups = [group(d) for _, d in DIL_CONFIGS]
    params = inv_perms + [attn_norm_g.reshape(1, -1), w_zg, b_gate.reshape(1, -1), w_o_mla.astype(BF16),
                          w_o_dil.astype(BF16), w_out.astype(BF16), final_norm_g.reshape(1, -1)]
    return pl.pallas_call(
        functools.partial(_out_kernel, final_norm=final_norm),
        grid=(B, S // tm),
        in_specs=[rows(D_MODEL), pl.BlockSpec((None, MLA_WIDTH, tm), lambda b, i: (b, 0, i))]
        + groups + groups + [full(a) for a in params],
        out_specs=rows(D_MODEL),
        out_shape=jax.ShapeDtypeStruct((B, S, D_MODEL), F32),
        compiler_params=pltpu.CompilerParams(dimension_semantics=("parallel", "parallel"),
                                             vmem_limit_bytes=VMEM_LIMIT_BYTES),
        name="out",
    )(x3, a_out, *os, *lses, *params)


def kernel(x, positions, attn_norm_g, w_in, b_gate, mla_q_norm_g, mla_kv_norm_g, w_uq, w_ukv,
           w_o_mla, w_o_dil, w_out, final_norm_g):
    B, S, _ = x.shape
    depth = w_in.shape[0]
    h = x
    pos3 = positions.reshape(B, S, 1)
    for layer in range(depth):
        (qT, k, vT), (qd, kd, vd) = _projection(
            h, pos3, attn_norm_g[layer], w_in[layer], mla_q_norm_g[layer], mla_kv_norm_g[layer],
            w_uq[layer], w_ukv[layer])
        a_out = _mla_attention(qT, k, vT)

        os, lses = [], []
        for g, (window, dilation) in enumerate(DIL_CONFIGS):
            o_g, lse_g = _dilated_group(qd[g], kd[g], vd[g], window // (2 * dilation))
            os.append(o_g)
            lses.append(lse_g)
        h = _output(h, a_out, os, lses, attn_norm_g[layer], w_in[layer], b_gate[layer],
                    w_o_mla[layer], w_o_dil[layer], w_out[layer], final_norm_g,
                    final_norm=(layer == depth - 1))
    return h
```

```python
import functools

import jax
import jax.numpy as jnp
from jax import lax
from jax.experimental import pallas as pl
from jax.experimental.pallas import tpu as pltpu

D_MODEL = 1024
ROPE_THETA = 10000.0
RMS_EPS = 1e-6
NEG_INF = -1e30

MLA_HEADS = 8
MLA_Q_RANK = 256
MLA_KV_RANK = 128
MLA_NOPE_DIM = 64
MLA_ROPE_DIM = 32
MLA_V_DIM = 64
MLA_QK_DIM = MLA_NOPE_DIM + MLA_ROPE_DIM
MLA_WIDTH = MLA_HEADS * MLA_V_DIM

DIL_CONFIGS = ((128, 1), (512, 4), (2048, 16))
DIL_GROUPS = len(DIL_CONFIGS)
DIL_HEADS_PER_GROUP = 8
DIL_HEAD_DIM = 64
DIL_QKV_WIDTH = DIL_GROUPS * DIL_HEADS_PER_GROUP * DIL_HEAD_DIM
DIL_WIDTH = DIL_HEADS_PER_GROUP * DIL_HEAD_DIM

IN_SPLITS = (MLA_Q_RANK, MLA_KV_RANK, MLA_ROPE_DIM, MLA_WIDTH,
             DIL_QKV_WIDTH, DIL_QKV_WIDTH, DIL_QKV_WIDTH, DIL_WIDTH,
             D_MODEL, D_MODEL)

LANES = 128
BF16_ROWS = 16
MLA_HEAD_PAD = LANES
VMEM_LIMIT_BYTES = 56 * 1024 * 1024

TOKEN_TILE = 512
PERM_BLOCK = 256
MLA_Q_TILE = 512
MLA_Q_TILES_PER_STEP = 2
MLA_K_TILE = 1024
DIL_Q_TILE = 256
DIL_TILES_PER_STEP = 4

BF16 = jnp.bfloat16
F32 = jnp.float32
LOG2_E = 1.4426950408889634
LN_2 = 0.6931471805599453

_NT = (((1,), (1,)), ((), ()))
_TN = (((0,), (0,)), ((), ()))


def _dot(a, b):
    return jnp.dot(a, b, preferred_element_type=F32)


def _rms(x, g):
    return x * lax.rsqrt(jnp.mean(x * x, axis=-1, keepdims=True) + RMS_EPS) * g


def _rope_tile(x, cos, sin_signed, first_half, half):
    rot = jnp.where(first_half, pltpu.roll(x, LANES - half, axis=1), pltpu.roll(x, half, axis=1))
    return x * cos + rot * sin_signed


def _split_offsets():
    o = [0]
    for s in IN_SPLITS:
        o.append(o[-1] + s)
    return o


def _residue_perm(tm, d):
    rows = jnp.arange(tm)
    src = (rows % (tm // d)) * d + rows // (tm // d)
    return (src[:, None] == jnp.arange(tm)[None, :]).astype(BF16)


def _proj_kernel(x_ref, pos_ref, gattn_ref, wmla_ref, wdil_ref, gq_ref, gkv_ref,
                 wuq_ref, wuk_ref, wuvT_ref, tab_ref, *rest):
    perm_refs = rest[:DIL_GROUPS - 1]
    qT_ref, k_ref, vT_ref = rest[DIL_GROUPS - 1:DIL_GROUPS + 2]
    dil_refs = rest[DIL_GROUPS + 2:]
    x = x_ref[...]
    hb = _rms(x, gattn_ref[...]).astype(BF16)
    pos = pos_ref[...].astype(F32)

    tab = tab_ref[...]
    lane = lax.broadcasted_iota(jnp.int32, (1, LANES), 1)

    ang_m = pos * tab[0:1, :]
    cos_m, sin_m = jnp.cos(ang_m), jnp.sin(ang_m) * tab[1:2, :]
    first_m = lane < (MLA_NOPE_DIM + MLA_ROPE_DIM // 2)
    q_scale = MLA_QK_DIM ** -0.5 * LOG2_E

    mla = _dot(hb, wmla_ref[...])
    c_q = mla[:, :MLA_Q_RANK]
    c_kv = mla[:, MLA_Q_RANK:MLA_Q_RANK + MLA_KV_RANK]
    kpe = mla[:, MLA_Q_RANK + MLA_KV_RANK:]
    kpe = _rope_tile(kpe, cos_m, sin_m, first_m, MLA_ROPE_DIM // 2)

    cqn = _rms(c_q, gq_ref[...]).astype(BF16)
    ckn = _rms(c_kv, gkv_ref[...]).astype(BF16)
    q = _dot(cqn, wuq_ref[...])
    kn = _dot(ckn, wuk_ref[...])
    vT_ref[...] = lax.dot_general(wuvT_ref[...], ckn, _NT,
                                  preferred_element_type=F32).astype(BF16)
    cos_q, sin_q = cos_m * q_scale, sin_m * q_scale
    for h in range(MLA_HEADS):
        sl = slice(h * MLA_HEAD_PAD, (h + 1) * MLA_HEAD_PAD)
        qh = _rope_tile(q[:, sl], cos_q, sin_q, first_m, MLA_ROPE_DIM // 2)
        qT_ref[sl, :] = qh.T.astype(BF16)
        k_ref[:, sl] = (kn[:, sl] + kpe).astype(BF16)

    ang_d = pos * tab[2:3, :]
    cos_d, sin_d = jnp.cos(ang_d), jnp.sin(ang_d) * tab[3:4, :]
    first_d = (lane % DIL_HEAD_DIM) < (DIL_HEAD_DIM // 2)
    qd_scale = DIL_HEAD_DIM ** -0.5 * LOG2_E
    tiles_per_group = DIL_WIDTH // LANES
    for part in range(3):
        y = _dot(hb, wdil_ref[:, part * DIL_QKV_WIDTH:(part + 1) * DIL_QKV_WIDTH])
        cos_p, sin_p = (cos_d * qd_scale, sin_d * qd_scale) if part == 0 else (cos_d, sin_d)
        for g in range(DIL_GROUPS):
            cols = []
            for t in range(g * tiles_per_group, (g + 1) * tiles_per_group):
                yt = y[:, t * LANES:(t + 1) * LANES]
                if part < 2:
                    yt = _rope_tile(yt, cos_p, sin_p, first_d, DIL_HEAD_DIM // 2)
                cols.append(yt.astype(BF16))
            yg = jnp.concatenate(cols, axis=1)
            out_ref = dil_refs[part * DIL_GROUPS + g]
            d = out_ref.shape[0]
            if d == 1:
                out_ref[...] = yg.reshape(out_ref.shape)
                continue
            per = PERM_BLOCK // d
            for blk in range(yg.shape[0] // PERM_BLOCK):
                rows = yg[blk * PERM_BLOCK:(blk + 1) * PERM_BLOCK]
                gathered = _dot(perm_refs[g - 1][...], rows).astype(BF16)
                out_ref[:, blk * per:(blk + 1) * per, :] = gathered.reshape(d, per, DIL_WIDTH)


def _rope_tables():
    lane = jnp.arange(LANES)
    inv_m = ROPE_THETA ** (-jnp.arange(0, MLA_ROPE_DIM, 2, dtype=F32) / MLA_ROPE_DIM)
    inv_d = ROPE_THETA ** (-jnp.arange(0, DIL_HEAD_DIM, 2, dtype=F32) / DIL_HEAD_DIM)
    in_rope = (lane >= MLA_NOPE_DIM) & (lane < MLA_QK_DIM)
    r = lane - MLA_NOPE_DIM
    tab_m = jnp.where(in_rope, inv_m[jnp.clip(r, 0, MLA_ROPE_DIM - 1) % (MLA_ROPE_DIM // 2)], 0.0)
    sgn_m = jnp.where(in_rope, jnp.where(r < MLA_ROPE_DIM // 2, -1.0, 1.0), 0.0)
    e = lane % DIL_HEAD_DIM
    tab_d = inv_d[e % (DIL_HEAD_DIM // 2)]
    sgn_d = jnp.where(e < DIL_HEAD_DIM // 2, -1.0, 1.0)
    zeros = jnp.zeros((LANES,), F32)
    return jnp.stack([tab_m, sgn_m, tab_d, sgn_d, zeros, zeros, zeros, zeros]).astype(F32)


def _projection(x3, pos3, attn_norm_g, w_in, gq, gkv, w_uq, w_ukv):
    B, S, _ = x3.shape
    tm = TOKEN_TILE
    o = _split_offsets()
    w_cq, w_ckv, w_kr = w_in[:, o[0]:o[1]], w_in[:, o[1]:o[2]], w_in[:, o[2]:o[3]]
    kpe_slot = jnp.zeros((D_MODEL, LANES), F32).at[:, MLA_NOPE_DIM:MLA_QK_DIM].set(w_kr)
    w_mla = jnp.concatenate([w_cq, w_ckv, kpe_slot], axis=1).astype(BF16)
    w_dil = w_in[:, o[4]:o[7]].astype(BF16)
    wq = w_uq.reshape(MLA_Q_RANK, MLA_HEADS, MLA_QK_DIM)
    wq = jnp.pad(wq, ((0, 0), (0, 0), (0, MLA_HEAD_PAD - MLA_QK_DIM)))
    wq = wq.reshape(MLA_Q_RANK, MLA_HEADS * MLA_HEAD_PAD).astype(BF16)
    wkv = w_ukv.reshape(MLA_KV_RANK, MLA_HEADS, MLA_NOPE_DIM + MLA_V_DIM)
    wk = jnp.pad(wkv[:, :, :MLA_NOPE_DIM], ((0, 0), (0, 0), (0, MLA_HEAD_PAD - MLA_NOPE_DIM)))
    wk = wk.reshape(MLA_KV_RANK, MLA_HEADS * MLA_HEAD_PAD).astype(BF16)
    wv = wkv[:, :, MLA_NOPE_DIM:].reshape(MLA_KV_RANK, MLA_WIDTH).T.astype(BF16)
    perms = [_residue_perm(PERM_BLOCK, d) for _, d in DIL_CONFIGS[1:]]

    full = lambda a: pl.BlockSpec(a.shape, lambda b, i: (0,) * a.ndim, pipeline_mode=pl.Buffered(1))
    rows = lambda w: pl.BlockSpec((None, tm, w), lambda b, i: (b, i, 0))
    cols = lambda h: pl.BlockSpec((None, h, tm), lambda b, i: (b, 0, i))
    params = [attn_norm_g.reshape(1, -1), w_mla, w_dil, gq.reshape(1, -1), gkv.reshape(1, -1),
              wq, wk, wv, _rope_tables()] + perms
    qk_w = MLA_HEADS * MLA_HEAD_PAD
    out_specs = [cols(qk_w), rows(qk_w), cols(MLA_WIDTH)]
    out_shape = [jax.ShapeDtypeStruct((B, qk_w, S), BF16), jax.ShapeDtypeStruct((B, S, qk_w), BF16),
                 jax.ShapeDtypeStruct((B, MLA_WIDTH, S), BF16)]
    for _ in range(3):
        for _, d in DIL_CONFIGS:
            out_specs.append(pl.BlockSpec((None, d, tm // d, DIL_WIDTH), lambda b, i: (b, 0, i, 0)))
            out_shape.append(jax.ShapeDtypeStruct((B, d, S // d, DIL_WIDTH), BF16))
    outs = pl.pallas_call(
        _proj_kernel,
        grid=(B, S // tm),
        in_specs=[rows(D_MODEL), rows(1)] + [full(a) for a in params],
        out_specs=out_specs,
        out_shape=out_shape,
        compiler_params=pltpu.CompilerParams(dimension_semantics=("parallel", "parallel"),
                                             vmem_limit_bytes=VMEM_LIMIT_BYTES),
        name="proj",
    )(x3, pos3, *params)
    return outs[:3], [outs[3 + p * DIL_GROUPS:3 + (p + 1) * DIL_GROUPS] for p in range(3)]


def _mla_attn_kernel(qT_ref, k_ref, vT_ref, oT_ref, s0_ref, s1_ref, *, tq, tk):
    n_tiles = qT_ref.shape[1] // tq
    nk = k_ref.shape[0] // tk

    def scores(step):
        t, j = divmod(step, nk)
        qT = qT_ref[:, t * tq:(t + 1) * tq]
        return _dot(k_ref[j * tk:(j + 1) * tk, :], qT)

    def update(s_ref, j, carry):
        m, l, acc = carry
        s = s_ref[...]
        m_new = jnp.maximum(m, jnp.max(s, axis=0, keepdims=True))
        alpha = jnp.exp2(m - m_new)
        p = jnp.exp2(s - m_new)
        l = alpha * l + jnp.sum(p, axis=0, keepdims=True)
        acc = alpha * acc + _dot(vT_ref[:, j * tk:(j + 1) * tk], p.astype(BF16))
        return m_new, l, acc

    bufs = (s0_ref, s1_ref)
    s0_ref[...] = scores(0)
    for t in range(n_tiles):
        carry = (jnp.full((1, tq), -jnp.inf, F32), jnp.zeros((1, tq), F32),
                 jnp.zeros((MLA_V_DIM, tq), F32))
        for j in range(nk):
            step = t * nk + j
            if step + 1 < n_tiles * nk:
                bufs[(step + 1) % 2][...] = scores(step + 1)
            carry = update(bufs[step % 2], j, carry)
        _, l, acc = carry
        oT_ref[:, t * tq:(t + 1) * tq] = (acc / l).astype(oT_ref.dtype)


def _mla_attention(qT, k, vT):
    B, _, S = qT.shape
    tq, tk = MLA_Q_TILE, MLA_K_TILE
    cols = MLA_Q_TILES_PER_STEP * tq
    return pl.pallas_call(
        functools.partial(_mla_attn_kernel, tq=tq, tk=tk),
        grid=(B, MLA_HEADS, S // cols),
        scratch_shapes=[pltpu.VMEM((tk, tq), F32), pltpu.VMEM((tk, tq), F32)],
        in_specs=[
            pl.BlockSpec((None, MLA_HEAD_PAD, cols), lambda b, h, i: (b, h, i)),
            pl.BlockSpec((None, S, MLA_HEAD_PAD), lambda b, h, i: (b, 0, h)),
            pl.BlockSpec((None, MLA_V_DIM, S), lambda b, h, i: (b, h, 0)),
        ],
        out_specs=pl.BlockSpec((None, MLA_V_DIM, cols), lambda b, h, i: (b, h, i)),
        out_shape=jax.ShapeDtypeStruct((B, MLA_WIDTH, S), BF16),
        compiler_params=pltpu.CompilerParams(
            dimension_semantics=("parallel", "parallel", "parallel"),
            vmem_limit_bytes=VMEM_LIMIT_BYTES),
        name="mla_attn",
    )(qT, k, vT)


def _dil_attn_kernel(q_ref, k_ref, v_ref, o_ref, lse_ref, s0_ref, s1_ref, *, tl, side):
    L = k_ref.shape[0]
    win = tl + 2 * side
    n_tiles = q_ref.shape[0] // tl
    heads = DIL_HEADS_PER_GROUP
    c = lax.broadcasted_iota(jnp.int32, (win, tl), 0)
    a = lax.broadcasted_iota(jnp.int32, (win, tl), 1)
    lane = lax.broadcasted_iota(jnp.int32, (1, LANES), 1)
    lo = lane < DIL_HEAD_DIM
    lanes_of = lambda pair: slice(pair * LANES, (pair + 1) * LANES)

    tiles = []
    for t in range(n_tiles):
        q0 = (pl.program_id(2) * n_tiles + t) * tl
        ks = pl.multiple_of(jnp.clip(q0 - side, 0, L - win), side)
        bias = jnp.where(jnp.abs((a - c) + (q0 - ks)) <= side, 0.0, NEG_INF)
        tiles.append((slice(t * tl, (t + 1) * tl), ks, bias))

    def scores(item):
        (rows, ks, bias), (pair, hh) = tiles[item // heads], divmod(item % heads, 2)
        qp = q_ref[rows, lanes_of(pair)]
        kp = k_ref[pl.ds(ks, win), lanes_of(pair)]
        qh = jnp.where(lo if hh == 0 else ~lo, qp, jnp.zeros_like(qp))
        return lax.dot_general(kp, qh, _NT, preferred_element_type=F32) + bias

    bufs = (s0_ref, s1_ref)
    s0_ref[...] = scores(0)
    for t, (rows, ks, _) in enumerate(tiles):
        for pair in range(heads // 2):
            sl = lanes_of(pair)
            vp = v_ref[pl.ds(ks, win), sl]
            o_rows, lse_rows = [], []
            for hh in range(2):
                item = t * heads + 2 * pair + hh
                if item + 1 < n_tiles * heads:
                    bufs[(item + 1) % 2][...] = scores(item + 1)
                s = bufs[item % 2][...]
                m = jnp.max(s, axis=0, keepdims=True)
                p = jnp.exp2(s - m)
                den = jnp.sum(p, axis=0, keepdims=True)
                oT = lax.dot_general(vp, p.astype(BF16), _TN, preferred_element_type=F32)
                o_rows.append(oT[hh * DIL_HEAD_DIM:(hh + 1) * DIL_HEAD_DIM] / den)
                lse_rows.append(jnp.broadcast_to(m * LN_2 + jnp.log(den), (DIL_HEAD_DIM, tl)))
            o_ref[rows, sl] = jnp.concatenate(o_rows, axis=0).T.astype(o_ref.dtype)
            lse_ref[rows, sl] = jnp.concatenate(lse_rows, axis=0).T


def _dilated_group(q, k, v, side):
    B, d, L, _ = q.shape
    tl = DIL_Q_TILE
    while tl + 2 * side > L:
        tl //= 2
    rows = min(DIL_TILES_PER_STEP * tl, L)
    q_spec = pl.BlockSpec((None, None, rows, DIL_WIDTH), lambda b, r, n: (b, r, n, 0))
    kv_bytes = L * DIL_WIDTH * 2
    kv_mode = pl.Buffered(1) if 4 * kv_bytes > VMEM_LIMIT_BYTES // 4 else pl.Buffered(2)
    kv_spec = pl.BlockSpec((None, None, L, DIL_WIDTH), lambda b, r, n: (b, r, 0, 0),
                           pipeline_mode=kv_mode)
    return pl.pallas_call(
        functools.partial(_dil_attn_kernel, tl=tl, side=side),
        grid=(B, d, L // rows),
        scratch_shapes=[pltpu.VMEM((tl + 2 * side, tl), F32)] * 2,
        in_specs=[q_spec, kv_spec, kv_spec],
        out_specs=[q_spec, q_spec],
        out_shape=[jax.ShapeDtypeStruct(q.shape, BF16), jax.ShapeDtypeStruct(q.shape, F32)],
        compiler_params=pltpu.CompilerParams(
            dimension_semantics=("parallel", "parallel", "arbitrary"),
            vmem_limit_bytes=VMEM_LIMIT_BYTES),
        name=f"dil_attn_d{d}",
    )(q, k, v)


def _exact_gather_f32(perm, x):
    hi = x.astype(BF16)
    r1 = x - hi.astype(F32)
    mid = r1.astype(BF16)
    lo = (r1 - mid.astype(F32)).astype(BF16)
    return _dot(perm, hi) + _dot(perm, mid) + _dot(perm, lo)


def _out_kernel(x_ref, a_ref, o0_ref, o1_ref, o2_ref, l0_ref, l1_ref, l2_ref, p1_ref, p2_ref,
                gattn_ref, wzg_ref, bg_ref, womla_ref, wodil_ref, wout_ref, gfin_ref, out_ref,
                *, final_norm):
    x = x_ref[...]
    tm = x.shape[0]
    hb = _rms(x, gattn_ref[...]).astype(BF16)
    zg = _dot(hb, wzg_ref[...])
    z_mla = zg[:, :MLA_WIDTH]
    z_dil = zg[:, MLA_WIDTH:MLA_WIDTH + DIL_WIDTH]
    g = zg[:, MLA_WIDTH + DIL_WIDTH:] + bg_ref[...]
    g_mla, g_dil = g[:, :D_MODEL], g[:, D_MODEL:]

    a = a_ref[...].astype(F32).T
    y_mla = _dot((a * jax.nn.silu(z_mla)).astype(BF16), womla_ref[...])

    def token_order(ref, perm_ref, gather):
        d = ref.shape[0]
        per = PERM_BLOCK // d
        blocks = [gather(perm_ref[...], ref[:, b * per:(b + 1) * per, :].reshape(PERM_BLOCK, DIL_WIDTH))
                  for b in range(tm // PERM_BLOCK)]
        return jnp.concatenate(blocks, axis=0)

    o0 = o0_ref[...].reshape(tm, DIL_WIDTH).astype(F32)
    l0 = l0_ref[...].reshape(tm, DIL_WIDTH)
    o1 = token_order(o1_ref, p1_ref, _dot)
    l1 = token_order(l1_ref, p1_ref, _exact_gather_f32)
    o2 = token_order(o2_ref, p2_ref, _dot)
    l2 = token_order(l2_ref, p2_ref, _exact_gather_f32)
    mx = jnp.maximum(jnp.maximum(l0, l1), l2)
    e0, e1, e2 = jnp.exp(l0 - mx), jnp.exp(l1 - mx), jnp.exp(l2 - mx)
    den = e0 + e1 + e2
    d = e0 / den * o0 + e1 / den * o1 + e2 / den * o2
    y_dil = _dot((d * jax.nn.silu(z_dil)).astype(BF16), wodil_ref[...])

    merged = jax.nn.sigmoid(g_mla) * y_mla + jax.nn.sigmoid(g_dil) * y_dil
    y = x + _dot(merged.astype(BF16), wout_ref[...])
    out_ref[...] = _rms(y, gfin_ref[...]) if final_norm else y


def _output(x3, a_out, os, lses, attn_norm_g, w_in, b_gate, w_o_mla, w_o_dil, w_out, final_norm_g, *,
            final_norm):
    B, S, _ = x3.shape
    tm = TOKEN_TILE
    o = _split_offsets()
    w_zg = jnp.concatenate([w_in[:, o[3]:o[4]], w_in[:, o[7]:o[10]]], axis=1).astype(BF16)
    inv_perms = [_residue_perm(PERM_BLOCK, d).T for _, d in DIL_CONFIGS[1:]]
    full = lambda a: pl.BlockSpec(a.shape, lambda b, i: (0,) * a.ndim, pipeline_mode=pl.Buffered(1))
    rows = lambda w: pl.BlockSpec((None, tm, w), lambda b, i: (b, i, 0))
    group = lambda d: pl.BlockSpec((None, d, tm // d, DIL_WIDTH), lambda b, i: (b, 0, i, 0))
    groups = [group(d) for _, d in DIL_CONFIGS]
    params = inv_perms + [attn_norm_g.reshape(1, -1), w_zg, b_gate.reshape(1, -1), w_o_mla.astype(BF16),
                          w_o_dil.astype(BF16), w_out.astype(BF16), final_norm_g.reshape(1, -1)]
    return pl.pallas_call(
        functools.partial(_out_kernel, final_norm=final_norm),
        grid=(B, S // tm),
        in_specs=[rows(D_MODEL), pl.BlockSpec((None, MLA_WIDTH, tm), lambda b, i: (b, 0, i))]
        + groups + groups + [full(a) for a in params],
        out_specs=rows(D_MODEL),
        out_shape=jax.ShapeDtypeStruct((B, S, D_MODEL), F32),
        compiler_params=pltpu.CompilerParams(dimension_semantics=("parallel", "parallel"),
                                             vmem_limit_bytes=VMEM_LIMIT_BYTES),
        name="out",
    )(x3, a_out, *os, *lses, *params)


def kernel(x, positions, attn_norm_g, w_in, b_gate, mla_q_norm_g, mla_kv_norm_g, w_uq, w_ukv,
           w_o_mla, w_o_dil, w_out, final_norm_g):
    B, S, _ = x.shape
    depth = w_in.shape[0]
    h = x
    pos3 = positions.reshape(B, S, 1)
    for layer in range(depth):
        (qT, k, vT), (qd, kd, vd) = _projection(
            h, pos3, attn_norm_g[layer], w_in[layer], mla_q_norm_g[layer], mla_kv_norm_g[layer],
            w_uq[layer], w_ukv[layer])
        a_out = _mla_attention(qT, k, vT)

        os, lses = [], []
        for g, (window, dilation) in enumerate(DIL_CONFIGS):
            o_g, lse_g = _dilated_group(qd[g], kd[g], vd[g], window // (2 * dilation))
            os.append(o_g)
            lses.append(lse_g)
        h = _output(h, a_out, os, lses, attn_norm_g[layer], w_in[layer], b_gate[layer],
                    w_o_mla[layer], w_o_dil[layer], w_out[layer], final_norm_g,
                    final_norm=(layer == depth - 1))
    return h
```

```python
import functools

import jax
import jax.numpy as jnp
from jax import lax
from jax.experimental import pallas as pl
from jax.experimental.pallas import tpu as pltpu

D_MODEL = 1024
ROPE_THETA = 10000.0
RMS_EPS = 1e-6
NEG_INF = -1e30

MLA_HEADS = 8
MLA_Q_RANK = 256
MLA_KV_RANK = 128
MLA_NOPE_DIM = 64
MLA_ROPE_DIM = 32
MLA_V_DIM = 64
MLA_QK_DIM = MLA_NOPE_DIM + MLA_ROPE_DIM
MLA_WIDTH = MLA_HEADS * MLA_V_DIM

DIL_CONFIGS = ((128, 1), (512, 4), (2048, 16))
DIL_GROUPS = len(DIL_CONFIGS)
DIL_HEADS_PER_GROUP = 8
DIL_HEAD_DIM = 64
DIL_QKV_WIDTH = DIL_GROUPS * DIL_HEADS_PER_GROUP * DIL_HEAD_DIM
DIL_WIDTH = DIL_HEADS_PER_GROUP * DIL_HEAD_DIM

IN_SPLITS = (MLA_Q_RANK, MLA_KV_RANK, MLA_ROPE_DIM, MLA_WIDTH,
             DIL_QKV_WIDTH, DIL_QKV_WIDTH, DIL_QKV_WIDTH, DIL_WIDTH,
             D_MODEL, D_MODEL)

LANES = 128
BF16_ROWS = 16
MLA_HEAD_PAD = LANES
VMEM_LIMIT_BYTES = 56 * 1024 * 1024

TOKEN_TILE = 512
STAGE_SLABS = 8
MLA_Q_TILE = 512
MLA_Q_TILES_PER_STEP = 2
MLA_K_TILE = 1024
DIL_Q_TILE = 256
DIL_TILES_PER_STEP = 4

BF16 = jnp.bfloat16
F32 = jnp.float32
LOG2_E = 1.4426950408889634
LN_2 = 0.6931471805599453

_NT = (((1,), (1,)), ((), ()))
_TN = (((0,), (0,)), ((), ()))


def _dot(a, b):
    return jnp.dot(a, b, preferred_element_type=F32)


def _rms(x, g):
    return x * lax.rsqrt(jnp.mean(x * x, axis=-1, keepdims=True) + RMS_EPS) * g


def _rope_tile(x, cos, sin_signed, first_half, half):
    rot = jnp.where(first_half, pltpu.roll(x, LANES - half, axis=1), pltpu.roll(x, half, axis=1))
    return x * cos + rot * sin_signed


def _split_offsets():
    o = [0]
    for s in IN_SPLITS:
        o.append(o[-1] + s)
    return o


def _proj_kernel(x_ref, pos_ref, gattn_ref, wmla_ref, wdil_ref, gq_ref, gkv_ref,
                 wuq_ref, wuk_ref, wuvT_ref, tab_ref, qT_ref, k_ref, vT_ref, *rest):
    dil_refs, stage_ref = rest[:-1], rest[-1]
    x = x_ref[...]
    tm = x.shape[0]
    hb = _rms(x, gattn_ref[...]).astype(BF16)
    pos = pos_ref[...].astype(F32)

    tab = tab_ref[...]
    lane = lax.broadcasted_iota(jnp.int32, (1, LANES), 1)

    ang_m = pos * tab[0:1, :]
    cos_m, sin_m = jnp.cos(ang_m), jnp.sin(ang_m) * tab[1:2, :]
    first_m = lane < (MLA_NOPE_DIM + MLA_ROPE_DIM // 2)
    q_scale = MLA_QK_DIM ** -0.5 * LOG2_E

    mla = _dot(hb, wmla_ref[...])
    c_q = mla[:, :MLA_Q_RANK]
    c_kv = mla[:, MLA_Q_RANK:MLA_Q_RANK + MLA_KV_RANK]
    kpe = mla[:, MLA_Q_RANK + MLA_KV_RANK:]
    kpe = _rope_tile(kpe, cos_m, sin_m, first_m, MLA_ROPE_DIM // 2)

    cqn = _rms(c_q, gq_ref[...]).astype(BF16)
    ckn = _rms(c_kv, gkv_ref[...]).astype(BF16)
    q = _dot(cqn, wuq_ref[...])
    kn = _dot(ckn, wuk_ref[...])
    vT_ref[...] = lax.dot_general(wuvT_ref[...], ckn, _NT,
                                  preferred_element_type=F32).astype(BF16)
    cos_q, sin_q = cos_m * q_scale, sin_m * q_scale
    for h in range(MLA_HEADS):
        sl = slice(h * MLA_HEAD_PAD, (h + 1) * MLA_HEAD_PAD)
        qh = _rope_tile(q[:, sl], cos_q, sin_q, first_m, MLA_ROPE_DIM // 2)
        qT_ref[sl, :] = qh.T.astype(BF16)
        k_ref[:, sl] = (kn[:, sl] + kpe).astype(BF16)

    ang_d = pos * tab[2:3, :]
    cos_d, sin_d = jnp.cos(ang_d), jnp.sin(ang_d) * tab[3:4, :]
    first_d = (lane % DIL_HEAD_DIM) < (DIL_HEAD_DIM // 2)
    qd_scale = DIL_HEAD_DIM ** -0.5 * LOG2_E
    tiles_per_group = DIL_WIDTH // LANES
    n_staged = 0
    for part in range(3):
        y = _dot(hb, wdil_ref[:, part * DIL_QKV_WIDTH:(part + 1) * DIL_QKV_WIDTH])
        cos_p, sin_p = (cos_d * qd_scale, sin_d * qd_scale) if part == 0 else (cos_d, sin_d)
        for g in range(DIL_GROUPS):
            out_ref = dil_refs[part * DIL_GROUPS + g]
            d = out_ref.shape[0]
            for c in range(tiles_per_group):
                t = g * tiles_per_group + c
                yt = y[:, t * LANES:(t + 1) * LANES]
                if part < 2:
                    yt = _rope_tile(yt, cos_p, sin_p, first_d, DIL_HEAD_DIM // 2)
                lanes = slice(c * LANES, (c + 1) * LANES)
                if d == 1:
                    out_ref[0, :, lanes] = yt.astype(BF16)
                    continue
                slab = stage_ref.at[n_staged % stage_ref.shape[0]]
                n_staged += 1
                slab[...] = yt
                for r in range(d):
                    out_ref[r, :, lanes] = slab[pl.ds(r, tm // d, stride=d), :].astype(BF16)


def _rope_tables():
    lane = jnp.arange(LANES)
    inv_m = ROPE_THETA ** (-jnp.arange(0, MLA_ROPE_DIM, 2, dtype=F32) / MLA_ROPE_DIM)
    inv_d = ROPE_THETA ** (-jnp.arange(0, DIL_HEAD_DIM, 2, dtype=F32) / DIL_HEAD_DIM)
    in_rope = (lane >= MLA_NOPE_DIM) & (lane < MLA_QK_DIM)
    r = lane - MLA_NOPE_DIM
    tab_m = jnp.where(in_rope, inv_m[jnp.clip(r, 0, MLA_ROPE_DIM - 1) % (MLA_ROPE_DIM // 2)], 0.0)
    sgn_m = jnp.where(in_rope, jnp.where(r < MLA_ROPE_DIM // 2, -1.0, 1.0), 0.0)
    e = lane % DIL_HEAD_DIM
    tab_d = inv_d[e % (DIL_HEAD_DIM // 2)]
    sgn_d = jnp.where(e < DIL_HEAD_DIM // 2, -1.0, 1.0)
    zeros = jnp.zeros((LANES,), F32)
    return jnp.stack([tab_m, sgn_m, tab_d, sgn_d, zeros, zeros, zeros, zeros]).astype(F32)


def _projection(x3, pos3, attn_norm_g, w_in, gq, gkv, w_uq, w_ukv):
    B, S, _ = x3.shape
    tm = TOKEN_TILE
    o = _split_offsets()
    w_cq, w_ckv, w_kr = w_in[:, o[0]:o[1]], w_in[:, o[1]:o[2]], w_in[:, o[2]:o[3]]
    kpe_slot = jnp.zeros((D_MODEL, LANES), F32).at[:, MLA_NOPE_DIM:MLA_QK_DIM].set(w_kr)
    w_mla = jnp.concatenate([w_cq, w_ckv, kpe_slot], axis=1).astype(BF16)
    w_dil = w_in[:, o[4]:o[7]].astype(BF16)
    wq = w_uq.reshape(MLA_Q_RANK, MLA_HEADS, MLA_QK_DIM)
    wq = jnp.pad(wq, ((0, 0), (0, 0), (0, MLA_HEAD_PAD - MLA_QK_DIM)))
    wq = wq.reshape(MLA_Q_RANK, MLA_HEADS * MLA_HEAD_PAD).astype(BF16)
    wkv = w_ukv.reshape(MLA_KV_RANK, MLA_HEADS, MLA_NOPE_DIM + MLA_V_DIM)
    wk = jnp.pad(wkv[:, :, :MLA_NOPE_DIM], ((0, 0), (0, 0), (0, MLA_HEAD_PAD - MLA_NOPE_DIM)))
    wk = wk.reshape(MLA_KV_RANK, MLA_HEADS * MLA_HEAD_PAD).astype(BF16)
    wv = wkv[:, :, MLA_NOPE_DIM:].reshape(MLA_KV_RANK, MLA_WIDTH).T.astype(BF16)

    full = lambda a: pl.BlockSpec(a.shape, lambda b, i: (0,) * a.ndim, pipeline_mode=pl.Buffered(1))
    rows = lambda w: pl.BlockSpec((None, tm, w), lambda b, i: (b, i, 0))
    cols = lambda h: pl.BlockSpec((None, h, tm), lambda b, i: (b, 0, i))
    params = [attn_norm_g.reshape(1, -1), w_mla, w_dil, gq.reshape(1, -1), gkv.reshape(1, -1),
              wq, wk, wv, _rope_tables()]
    qk_w = MLA_HEADS * MLA_HEAD_PAD
    out_specs = [cols(qk_w), rows(qk_w), cols(MLA_WIDTH)]
    out_shape = [jax.ShapeDtypeStruct((B, qk_w, S), BF16), jax.ShapeDtypeStruct((B, S, qk_w), BF16),
                 jax.ShapeDtypeStruct((B, MLA_WIDTH, S), BF16)]
    for _ in range(3):
        for _, d in DIL_CONFIGS:
            out_specs.append(pl.BlockSpec((None, d, tm // d, DIL_WIDTH), lambda b, i: (b, 0, i, 0)))
            out_shape.append(jax.ShapeDtypeStruct((B, d, S // d, DIL_WIDTH), BF16))
    outs = pl.pallas_call(
        _proj_kernel,
        grid=(B, S // tm),
        in_specs=[rows(D_MODEL), rows(1)] + [full(a) for a in params],
        out_specs=out_specs,
        out_shape=out_shape,
        scratch_shapes=[pltpu.VMEM((STAGE_SLABS, tm, LANES), F32)],
        compiler_params=pltpu.CompilerParams(dimension_semantics=("parallel", "parallel"),
                                             vmem_limit_bytes=VMEM_LIMIT_BYTES),
        name="proj",
    )(x3, pos3, *params)
    return outs[:3], [outs[3 + p * DIL_GROUPS:3 + (p + 1) * DIL_GROUPS] for p in range(3)]


def _mla_attn_kernel(qT_ref, k_ref, vT_ref, oT_ref, s0_ref, s1_ref, *, tq, tk):
    n_tiles = qT_ref.shape[1] // tq
    nk = k_ref.shape[0] // tk

    def scores(step):
        t, j = divmod(step, nk)
        qT = qT_ref[:, t * tq:(t + 1) * tq]
        return _dot(k_ref[j * tk:(j + 1) * tk, :], qT)

    def update(s_ref, j, carry):
        m, l, acc = carry
        s = s_ref[...]
        m_new = jnp.maximum(m, jnp.max(s, axis=0, keepdims=True))
        alpha = jnp.exp2(m - m_new)
        p = jnp.exp2(s - m_new)
        l = alpha * l + jnp.sum(p, axis=0, keepdims=True)
        acc = alpha * acc + _dot(vT_ref[:, j * tk:(j + 1) * tk], p.astype(BF16))
        return m_new, l, acc

    bufs = (s0_ref, s1_ref)
    s0_ref[...] = scores(0)
    for t in range(n_tiles):
        carry = (jnp.full((1, tq), -jnp.inf, F32), jnp.zeros((1, tq), F32),
                 jnp.zeros((MLA_V_DIM, tq), F32))
        for j in range(nk):
            step = t * nk + j
            if step + 1 < n_tiles * nk:
                bufs[(step + 1) % 2][...] = scores(step + 1)
            carry = update(bufs[step % 2], j, carry)
        _, l, acc = carry
        oT_ref[:, t * tq:(t + 1) * tq] = (acc / l).astype(oT_ref.dtype)


def _mla_attention(qT, k, vT):
    B, _, S = qT.shape
    tq, tk = MLA_Q_TILE, MLA_K_TILE
    cols = MLA_Q_TILES_PER_STEP * tq
    return pl.pallas_call(
        functools.partial(_mla_attn_kernel, tq=tq, tk=tk),
        grid=(B, MLA_HEADS, S // cols),
        scratch_shapes=[pltpu.VMEM((tk, tq), F32), pltpu.VMEM((tk, tq), F32)],
        in_specs=[
            pl.BlockSpec((None, MLA_HEAD_PAD, cols), lambda b, h, i: (b, h, i)),
            pl.BlockSpec((None, S, MLA_HEAD_PAD), lambda b, h, i: (b, 0, h)),
            pl.BlockSpec((None, MLA_V_DIM, S), lambda b, h, i: (b, h, 0)),
        ],
        out_specs=pl.BlockSpec((None, MLA_V_DIM, cols), lambda b, h, i: (b, h, i)),
        out_shape=jax.ShapeDtypeStruct((B, MLA_WIDTH, S), BF16),
        compiler_params=pltpu.CompilerParams(
            dimension_semantics=("parallel", "parallel", "parallel"),
            vmem_limit_bytes=VMEM_LIMIT_BYTES),
        name="mla_attn",
    )(qT, k, vT)


def _dil_attn_kernel(q_ref, k_ref, v_ref, o_ref, lse_ref, s0_ref, s1_ref, *, tl, side):
    L = k_ref.shape[0]
    win = tl + 2 * side
    n_tiles = q_ref.shape[0] // tl
    heads = DIL_HEADS_PER_GROUP
    c = lax.broadcasted_iota(jnp.int32, (win, tl), 0)
    a = lax.broadcasted_iota(jnp.int32, (win, tl), 1)
    lane = lax.broadcasted_iota(jnp.int32, (1, LANES), 1)
    lo = lane < DIL_HEAD_DIM
    lanes_of = lambda pair: slice(pair * LANES, (pair + 1) * LANES)

    tiles = []
    for t in range(n_tiles):
        q0 = (pl.program_id(2) * n_tiles + t) * tl
        ks = pl.multiple_of(jnp.clip(q0 - side, 0, L - win), side)
        bias = jnp.where(jnp.abs((a - c) + (q0 - ks)) <= side, 0.0, NEG_INF)
        tiles.append((slice(t * tl, (t + 1) * tl), ks, bias))

    def scores(item):
        (rows, ks, bias), (pair, hh) = tiles[item // heads], divmod(item % heads, 2)
        qp = q_ref[rows, lanes_of(pair)]
        kp = k_ref[pl.ds(ks, win), lanes_of(pair)]
        qh = jnp.where(lo if hh == 0 else ~lo, qp, jnp.zeros_like(qp))
        return lax.dot_general(kp, qh, _NT, preferred_element_type=F32) + bias

    bufs = (s0_ref, s1_ref)
    s0_ref[...] = scores(0)
    for t, (rows, ks, _) in enumerate(tiles):
        for pair in range(heads // 2):
            sl = lanes_of(pair)
            vp = v_ref[pl.ds(ks, win), sl]
            o_rows, lse_rows = [], []
            for hh in range(2):
                item = t * heads + 2 * pair + hh
                if item + 1 < n_tiles * heads:
                    bufs[(item + 1) % 2][...] = scores(item + 1)
                s = bufs[item % 2][...]
                m = jnp.max(s, axis=0, keepdims=True)
                p = jnp.exp2(s - m)
                den = jnp.sum(p, axis=0, keepdims=True)
                oT = lax.dot_general(vp, p.astype(BF16), _TN, preferred_element_type=F32)
                o_rows.append(oT[hh * DIL_HEAD_DIM:(hh + 1) * DIL_HEAD_DIM] / den)
                lse_rows.append(jnp.broadcast_to(m * LN_2 + jnp.log(den), (DIL_HEAD_DIM, tl)))
            o_ref[rows, sl] = jnp.concatenate(o_rows, axis=0).T.astype(o_ref.dtype)
            lse_ref[rows, sl] = jnp.concatenate(lse_rows, axis=0).T


def _dilated_group(q, k, v, side):
    B, d, L, _ = q.shape
    tl = DIL_Q_TILE
    while tl + 2 * side > L:
        tl //= 2
    rows = min(DIL_TILES_PER_STEP * tl, L)
    q_spec = pl.BlockSpec((None, None, rows, DIL_WIDTH), lambda b, r, n: (b, r, n, 0))
    kv_bytes = L * DIL_WIDTH * 2
    kv_mode = pl.Buffered(1) if 4 * kv_bytes > VMEM_LIMIT_BYTES // 4 else pl.Buffered(2)
    kv_spec = pl.BlockSpec((None, None, L, DIL_WIDTH), lambda b, r, n: (b, r, 0, 0),
                           pipeline_mode=kv_mode)
    return pl.pallas_call(
        functools.partial(_dil_attn_kernel, tl=tl, side=side),
        grid=(B, d, L // rows),
        scratch_shapes=[pltpu.VMEM((tl + 2 * side, tl), F32)] * 2,
        in_specs=[q_spec, kv_spec, kv_spec],
        out_specs=[q_spec, q_spec],
        out_shape=[jax.ShapeDtypeStruct(q.shape, BF16), jax.ShapeDtypeStruct(q.shape, F32)],
        compiler_params=pltpu.CompilerParams(
            dimension_semantics=("parallel", "parallel", "arbitrary"),
            vmem_limit_bytes=VMEM_LIMIT_BYTES),
        name=f"dil_attn_d{d}",
    )(q, k, v)


def _out_kernel(x_ref, a_ref, o0_ref, o1_ref, o2_ref, l0_ref, l1_ref, l2_ref,
                gattn_ref, wzg_ref, bg_ref, womla_ref, wodil_ref, wout_ref, gfin_ref, out_ref,
                stage_ref, *, final_norm):
    x = x_ref[...]
    tm = x.shape[0]
    hb = _rms(x, gattn_ref[...]).astype(BF16)
    zg = _dot(hb, wzg_ref[...])
    z_mla = zg[:, :MLA_WIDTH]
    z_dil = zg[:, MLA_WIDTH:MLA_WIDTH + DIL_WIDTH]
    g = zg[:, MLA_WIDTH + DIL_WIDTH:] + bg_ref[...]
    g_mla, g_dil = g[:, :D_MODEL], g[:, D_MODEL:]

    a = a_ref[...].astype(F32).T
    y_mla = _dot((a * jax.nn.silu(z_mla)).astype(BF16), womla_ref[...])

    staged = [0]

    def token_order(ref):
        d = ref.shape[0]
        cols = []
        for c in range(DIL_WIDTH // LANES):
            slab = stage_ref.at[staged[0] % stage_ref.shape[0]]
            staged[0] += 1
            for r in range(d):
                slab[pl.ds(r, tm // d, stride=d), :] = ref[r, :, c * LANES:(c + 1) * LANES].astype(F32)
            cols.append(slab[...])
        return jnp.concatenate(cols, axis=1)

    o0 = o0_ref[...].reshape(tm, DIL_WIDTH).astype(F32)
    l0 = l0_ref[...].reshape(tm, DIL_WIDTH)
    o1, l1 = token_order(o1_ref), token_order(l1_ref)
    o2, l2 = token_order(o2_ref), token_order(l2_ref)
    mx = jnp.maximum(jnp.maximum(l0, l1), l2)
    e0, e1, e2 = jnp.exp(l0 - mx), jnp.exp(l1 - mx), jnp.exp(l2 - mx)
    den = e0 + e1 + e2
    d = e0 / den * o0 + e1 / den * o1 + e2 / den * o2
    y_dil = _dot((d * jax.nn.silu(z_dil)).astype(BF16), wodil_ref[...])

    merged = jax.nn.sigmoid(g_mla) * y_mla + jax.nn.sigmoid(g_dil) * y_dil
    y = x + _dot(merged.astype(BF16), wout_ref[...])
    out_ref[...] = _rms(y, gfin_ref[...]) if final_norm else y


def _output(x3, a_out, os, lses, attn_norm_g, w_in, b_gate, w_o_mla, w_o_dil, w_out, final_norm_g, *,
            final_norm):
    B, S, _ = x3.shape
    tm = TOKEN_TILE
    o = _split_offsets()
    w_zg = jnp.concatenate([w_in[:, o[3]:o[4]], w_in[:, o[7]:o[10]]], axis=1).astype(BF16)
    full = lambda a: pl.BlockSpec(a.shape, lambda b, i: (0,) * a.ndim, pipeline_mode=pl.Buffered(1))
    rows = lambda w: pl.BlockSpec((None, tm, w), lambda b, i: (b, i, 0))
    group = lambda d: pl.BlockSpec((None, d, tm // d, DIL_WIDTH), lambda b, i: (b, 0, i, 0))
    groups = [group(d) for _, d in DIL_CONFIGS]
    params = [attn_norm_g.reshape(1, -1), w_zg, b_gate.reshape(1, -1), w_o_mla.astype(BF16),
              w_o_dil.astype(BF16), w_out.astype(BF16), final_norm_g.reshape(1, -1)]
    return pl.pallas_call(
        functools.partial(_out_kernel, final_norm=final_norm),
        grid=(B, S // tm),
        in_specs=[rows(D_MODEL), pl.BlockSpec((None, MLA_WIDTH, tm), lambda b, i: (b, 0, i))]
        + groups + groups + [full(a) for a in params],
        out_specs=rows(D_MODEL),
        out_shape=jax.ShapeDtypeStruct((B, S, D_MODEL), F32),
        scratch_shapes=[pltpu.VMEM((STAGE_SLABS, tm, LANES), F32)],
        compiler_params=pltpu.CompilerParams(dimension_semantics=("parallel", "parallel"),
                                             vmem_limit_bytes=VMEM_LIMIT_BYTES),
        name="out",
    )(x3, a_out, *os, *lses, *params)


def kernel(x, positions, attn_norm_g, w_in, b_gate, mla_q_norm_g, mla_kv_norm_g, w_uq, w_ukv,
           w_o_mla, w_o_dil, w_out, final_norm_g):
    B, S, _ = x.shape
    depth = w_in.shape[0]
    h = x
    pos3 = positions.reshape(B, S, 1)
    for layer in range(depth):
        (qT, k, vT), (qd, kd, vd) = _projection(
            h, pos3, attn_norm_g[layer], w_in[layer], mla_q_norm_g[layer], mla_kv_norm_g[layer],
            w_uq[layer], w_ukv[layer])
        a_out = _mla_attention(qT, k, vT)

        os, lses = [], []
        for g, (window, dilation) in enumerate(DIL_CONFIGS):
            o_g, lse_g = _dilated_group(qd[g], kd[g], vd[g], window // (2 * dilation))
            os.append(o_g)
            lses.append(lse_g)
        h = _output(h, a_out, os, lses, attn_norm_g[layer], w_in[layer], b_gate[layer],
                    w_o_mla[layer], w_o_dil[layer], w_out[layer], final_norm_g,
                    final_norm=(layer == depth - 1))
    return h
```

```python
import functools

import jax
import jax.numpy as jnp
from jax import lax
from jax.experimental import pallas as pl
from jax.experimental.pallas import tpu as pltpu

D_MODEL = 1024
ROPE_THETA = 10000.0
RMS_EPS = 1e-6
NEG_INF = -1e30

MLA_HEADS = 8
MLA_Q_RANK = 256
MLA_KV_RANK = 128
MLA_NOPE_DIM = 64
MLA_ROPE_DIM = 32
MLA_V_DIM = 64
MLA_QK_DIM = MLA_NOPE_DIM + MLA_ROPE_DIM
MLA_WIDTH = MLA_HEADS * MLA_V_DIM

DIL_CONFIGS = ((128, 1), (512, 4), (2048, 16))
DIL_GROUPS = len(DIL_CONFIGS)
DIL_HEADS_PER_GROUP = 8
DIL_HEAD_DIM = 64
DIL_QKV_WIDTH = DIL_GROUPS * DIL_HEADS_PER_GROUP * DIL_HEAD_DIM
DIL_WIDTH = DIL_HEADS_PER_GROUP * DIL_HEAD_DIM

IN_SPLITS = (MLA_Q_RANK, MLA_KV_RANK, MLA_ROPE_DIM, MLA_WIDTH,
             DIL_QKV_WIDTH, DIL_QKV_WIDTH, DIL_QKV_WIDTH, DIL_WIDTH,
             D_MODEL, D_MODEL)

LANES = 128
BF16_ROWS = 16
MLA_HEAD_PAD = LANES
VMEM_LIMIT_BYTES = 56 * 1024 * 1024

TOKEN_TILE = 512
STAGE_SLABS = 8
MLA_Q_TILE = 512
MLA_Q_TILES_PER_STEP = 2
MLA_K_TILE = 1024
DIL_Q_TILE = 256
DIL_TILES_PER_STEP = 4

BF16 = jnp.bfloat16
F32 = jnp.float32
LOG2_E = 1.4426950408889634
LN_2 = 0.6931471805599453

_NT = (((1,), (1,)), ((), ()))
_TN = (((0,), (0,)), ((), ()))


def _dot(a, b):
    return jnp.dot(a, b, preferred_element_type=F32)


def _rms(x, g):
    return x * lax.rsqrt(jnp.mean(x * x, axis=-1, keepdims=True) + RMS_EPS) * g


def _rope_tile(x, cos, sin_signed, first_half, half):
    rot = jnp.where(first_half, pltpu.roll(x, LANES - half, axis=1), pltpu.roll(x, half, axis=1))
    return x * cos + rot * sin_signed


def _split_offsets():
    o = [0]
    for s in IN_SPLITS:
        o.append(o[-1] + s)
    return o


def _proj_kernel(x_ref, pos_ref, gattn_ref, wmla_ref, wdil_ref, gq_ref, gkv_ref,
                 wuq_ref, wuk_ref, wuvT_ref, tab_ref, qT_ref, k_ref, vT_ref, *rest):
    dil_refs, stage_ref = rest[:-1], rest[-1]
    x = x_ref[...]
    tm = x.shape[0]
    hb = _rms(x, gattn_ref[...]).astype(BF16)
    pos = pos_ref[...].astype(F32)

    tab = tab_ref[...]
    lane = lax.broadcasted_iota(jnp.int32, (1, LANES), 1)

    ang_m = pos * tab[0:1, :]
    cos_m, sin_m = jnp.cos(ang_m), jnp.sin(ang_m) * tab[1:2, :]
    first_m = lane < (MLA_NOPE_DIM + MLA_ROPE_DIM // 2)
    q_scale = MLA_QK_DIM ** -0.5 * LOG2_E

    mla = _dot(hb, wmla_ref[...])
    c_q = mla[:, :MLA_Q_RANK]
    c_kv = mla[:, MLA_Q_RANK:MLA_Q_RANK + MLA_KV_RANK]
    kpe = mla[:, MLA_Q_RANK + MLA_KV_RANK:]
    kpe = _rope_tile(kpe, cos_m, sin_m, first_m, MLA_ROPE_DIM // 2)

    cqn = _rms(c_q, gq_ref[...]).astype(BF16)
    ckn = _rms(c_kv, gkv_ref[...]).astype(BF16)
    q = _dot(cqn, wuq_ref[...])
    kn = _dot(ckn, wuk_ref[...])
    vT_ref[...] = lax.dot_general(wuvT_ref[...], ckn, _NT,
                                  preferred_element_type=F32).astype(BF16)
    cos_q, sin_q = cos_m * q_scale, sin_m * q_scale
    for h in range(MLA_HEADS):
        sl = slice(h * MLA_HEAD_PAD, (h + 1) * MLA_HEAD_PAD)
        qh = _rope_tile(q[:, sl], cos_q, sin_q, first_m, MLA_ROPE_DIM // 2)
        qT_ref[sl, :] = qh.T.astype(BF16)
        k_ref[:, sl] = (kn[:, sl] + kpe).astype(BF16)

    ang_d = pos * tab[2:3, :]
    cos_d, sin_d = jnp.cos(ang_d), jnp.sin(ang_d) * tab[3:4, :]
    first_d = (lane % DIL_HEAD_DIM) < (DIL_HEAD_DIM // 2)
    qd_scale = DIL_HEAD_DIM ** -0.5 * LOG2_E
    tiles_per_group = DIL_WIDTH // LANES
    n_staged = 0
    for part in range(3):
        y = _dot(hb, wdil_ref[:, part * DIL_QKV_WIDTH:(part + 1) * DIL_QKV_WIDTH])
        cos_p, sin_p = (cos_d * qd_scale, sin_d * qd_scale) if part == 0 else (cos_d, sin_d)
        for g in range(DIL_GROUPS):
            out_ref = dil_refs[part * DIL_GROUPS + g]
            d = out_ref.shape[0]
            for c in range(tiles_per_group):
                t = g * tiles_per_group + c
                yt = y[:, t * LANES:(t + 1) * LANES]
                if part < 2:
                    yt = _rope_tile(yt, cos_p, sin_p, first_d, DIL_HEAD_DIM // 2)
                lanes = slice(c * LANES, (c + 1) * LANES)
                if d == 1:
                    out_ref[0, :, lanes] = yt.astype(BF16)
                    continue
                slab = stage_ref.at[n_staged % stage_ref.shape[0]]
                n_staged += 1
                slab[...] = yt
                for r in range(d):
                    out_ref[r, :, lanes] = slab[pl.ds(r, tm // d, stride=d), :].astype(BF16)


def _rope_tables():
    lane = jnp.arange(LANES)
    inv_m = ROPE_THETA ** (-jnp.arange(0, MLA_ROPE_DIM, 2, dtype=F32) / MLA_ROPE_DIM)
    inv_d = ROPE_THETA ** (-jnp.arange(0, DIL_HEAD_DIM, 2, dtype=F32) / DIL_HEAD_DIM)
    in_rope = (lane >= MLA_NOPE_DIM) & (lane < MLA_QK_DIM)
    r = lane - MLA_NOPE_DIM
    tab_m = jnp.where(in_rope, inv_m[jnp.clip(r, 0, MLA_ROPE_DIM - 1) % (MLA_ROPE_DIM // 2)], 0.0)
    sgn_m = jnp.where(in_rope, jnp.where(r < MLA_ROPE_DIM // 2, -1.0, 1.0), 0.0)
    e = lane % DIL_HEAD_DIM
    tab_d = inv_d[e % (DIL_HEAD_DIM // 2)]
    sgn_d = jnp.where(e < DIL_HEAD_DIM // 2, -1.0, 1.0)
    zeros = jnp.zeros((LANES,), F32)
    return jnp.stack([tab_m, sgn_m, tab_d, sgn_d, zeros, zeros, zeros, zeros]).astype(F32)


def _projection(x3, pos3, attn_norm_g, w_in, gq, gkv, w_uq, w_ukv):
    B, S, _ = x3.shape
    tm = TOKEN_TILE
    o = _split_offsets()
    w_cq, w_ckv, w_kr = w_in[:, o[0]:o[1]], w_in[:, o[1]:o[2]], w_in[:, o[2]:o[3]]
    kpe_slot = jnp.zeros((D_MODEL, LANES), F32).at[:, MLA_NOPE_DIM:MLA_QK_DIM].set(w_kr)
    w_mla = jnp.concatenate([w_cq, w_ckv, kpe_slot], axis=1).astype(BF16)
    w_dil = w_in[:, o[4]:o[7]].astype(BF16)
    wq = w_uq.reshape(MLA_Q_RANK, MLA_HEADS, MLA_QK_DIM)
    wq = jnp.pad(wq, ((0, 0), (0, 0), (0, MLA_HEAD_PAD - MLA_QK_DIM)))
    wq = wq.reshape(MLA_Q_RANK, MLA_HEADS * MLA_HEAD_PAD).astype(BF16)
    wkv = w_ukv.reshape(MLA_KV_RANK, MLA_HEADS, MLA_NOPE_DIM + MLA_V_DIM)
    wk = jnp.pad(wkv[:, :, :MLA_NOPE_DIM], ((0, 0), (0, 0), (0, MLA_HEAD_PAD - MLA_NOPE_DIM)))
    wk = wk.reshape(MLA_KV_RANK, MLA_HEADS * MLA_HEAD_PAD).astype(BF16)
    wv = wkv[:, :, MLA_NOPE_DIM:].reshape(MLA_KV_RANK, MLA_WIDTH).T.astype(BF16)

    full = lambda a: pl.BlockSpec(a.shape, lambda b, i: (0,) * a.ndim, pipeline_mode=pl.Buffered(1))
    rows = lambda w: pl.BlockSpec((None, tm, w), lambda b, i: (b, i, 0))
    cols = lambda h: pl.BlockSpec((None, h, tm), lambda b, i: (b, 0, i))
    params = [attn_norm_g.reshape(1, -1), w_mla, w_dil, gq.reshape(1, -1), gkv.reshape(1, -1),
              wq, wk, wv, _rope_tables()]
    qk_w = MLA_HEADS * MLA_HEAD_PAD
    out_specs = [cols(qk_w), rows(qk_w), cols(MLA_WIDTH)]
    out_shape = [jax.ShapeDtypeStruct((B, qk_w, S), BF16), jax.ShapeDtypeStruct((B, S, qk_w), BF16),
                 jax.ShapeDtypeStruct((B, MLA_WIDTH, S), BF16)]
    for _ in range(3):
        for _, d in DIL_CONFIGS:
            out_specs.append(pl.BlockSpec((None, d, tm // d, DIL_WIDTH), lambda b, i: (b, 0, i, 0)))
            out_shape.append(jax.ShapeDtypeStruct((B, d, S // d, DIL_WIDTH), BF16))
    outs = pl.pallas_call(
        _proj_kernel,
        grid=(B, S // tm),
        in_specs=[rows(D_MODEL), rows(1)] + [full(a) for a in params],
        out_specs=out_specs,
        out_shape=out_shape,
        scratch_shapes=[pltpu.VMEM((STAGE_SLABS, tm, LANES), F32)],
        compiler_params=pltpu.CompilerParams(dimension_semantics=("parallel", "parallel"),
                                             vmem_limit_bytes=VMEM_LIMIT_BYTES),
        name="proj",
    )(x3, pos3, *params)
    return outs[:3], [outs[3 + p * DIL_GROUPS:3 + (p + 1) * DIL_GROUPS] for p in range(3)]


def _mla_attn_kernel(qT_ref, k_ref, vT_ref, oT_ref, s0_ref, s1_ref, *, tq, tk):
    n_tiles = qT_ref.shape[1] // tq
    nk = k_ref.shape[0] // tk

    def scores(step):
        t, j = divmod(step, nk)
        qT = qT_ref[:, t * tq:(t + 1) * tq]
        return _dot(k_ref[j * tk:(j + 1) * tk, :], qT)

    def update(s_ref, j, carry):
        m, l, acc = carry
        s = s_ref[...]
        m_new = jnp.maximum(m, jnp.max(s, axis=0, keepdims=True))
        alpha = jnp.exp2(m - m_new)
        p = jnp.exp2(s - m_new)
        l = alpha * l + jnp.sum(p, axis=0, keepdims=True)
        acc = alpha * acc + _dot(vT_ref[:, j * tk:(j + 1) * tk], p.astype(BF16))
        return m_new, l, acc

    bufs = (s0_ref, s1_ref)
    s0_ref[...] = scores(0)
    for t in range(n_tiles):
        carry = (jnp.full((1, tq), -jnp.inf, F32), jnp.zeros((1, tq), F32),
                 jnp.zeros((MLA_V_DIM, tq), F32))
        for j in range(nk):
            step = t * nk + j
            if step + 1 < n_tiles * nk:
                bufs[(step + 1) % 2][...] = scores(step + 1)
            carry = update(bufs[step % 2], j, carry)
        _, l, acc = carry
        oT_ref[:, t * tq:(t + 1) * tq] = (acc / l).astype(oT_ref.dtype)


def _mla_attention(qT, k, vT):
    B, _, S = qT.shape
    tq, tk = MLA_Q_TILE, MLA_K_TILE
    cols = MLA_Q_TILES_PER_STEP * tq
    return pl.pallas_call(
        functools.partial(_mla_attn_kernel, tq=tq, tk=tk),
        grid=(B, MLA_HEADS, S // cols),
        scratch_shapes=[pltpu.VMEM((tk, tq), F32), pltpu.VMEM((tk, tq), F32)],
        in_specs=[
            pl.BlockSpec((None, MLA_HEAD_PAD, cols), lambda b, h, i: (b, h, i)),
            pl.BlockSpec((None, S, MLA_HEAD_PAD), lambda b, h, i: (b, 0, h)),
            pl.BlockSpec((None, MLA_V_DIM, S), lambda b, h, i: (b, h, 0)),
        ],
        out_specs=pl.BlockSpec((None, MLA_V_DIM, cols), lambda b, h, i: (b, h, i)),
        out_shape=jax.ShapeDtypeStruct((B, MLA_WIDTH, S), BF16),
        compiler_params=pltpu.CompilerParams(
            dimension_semantics=("parallel", "parallel", "parallel"),
            vmem_limit_bytes=VMEM_LIMIT_BYTES),
        name="mla_attn",
    )(qT, k, vT)


def _dil_attn_kernel(q_ref, k_ref, v_ref, o_ref, lse_ref, s0_ref, s1_ref, *, tl, side):
    L = k_ref.shape[0]
    win = tl + 2 * side
    n_tiles = q_ref.shape[0] // tl
    heads = DIL_HEADS_PER_GROUP
    c = lax.broadcasted_iota(jnp.int32, (win, tl), 0)
    a = lax.broadcasted_iota(jnp.int32, (win, tl), 1)
    lane = lax.broadcasted_iota(jnp.int32, (1, LANES), 1)
    lo = lane < DIL_HEAD_DIM
    lanes_of = lambda pair: slice(pair * LANES, (pair + 1) * LANES)

    tiles = []
    for t in range(n_tiles):
        q0 = (pl.program_id(2) * n_tiles + t) * tl
        ks = pl.multiple_of(jnp.clip(q0 - side, 0, L - win), side)
        bias = jnp.where(jnp.abs((a - c) + (q0 - ks)) <= side, 0.0, NEG_INF)
        tiles.append((slice(t * tl, (t + 1) * tl), ks, bias))

    n_pairs = heads // 2

    def scores(item):
        (rows, ks, bias), pair = tiles[item // n_pairs], item % n_pairs
        qp = q_ref[rows, lanes_of(pair)]
        kp = k_ref[pl.ds(ks, win), lanes_of(pair)]
        zero = jnp.zeros_like(kp)
        k2 = jnp.concatenate([jnp.where(lo, kp, zero), jnp.where(lo, zero, kp)], axis=0)
        s2 = lax.dot_general(k2, qp, _NT, preferred_element_type=F32)
        return s2 + jnp.concatenate([bias, bias], axis=0)

    bufs = (s0_ref, s1_ref)
    s0_ref[...] = scores(0)
    for t, (rows, ks, _) in enumerate(tiles):
        for pair in range(n_pairs):
            sl = lanes_of(pair)
            vp = v_ref[pl.ds(ks, win), sl]
            o_rows, lse_rows = [], []
            item = t * n_pairs + pair
            if item + 1 < n_tiles * n_pairs:
                bufs[(item + 1) % 2][...] = scores(item + 1)
            for hh in range(2):
                s = bufs[item % 2][hh * win:(hh + 1) * win, :]
                m = jnp.max(s, axis=0, keepdims=True)
                p = jnp.exp2(s - m).astype(BF16)
                den_row = (1 - hh) * DIL_HEAD_DIM
                v_aug = jnp.where(lo if hh == 0 else ~lo, vp, (lane == den_row).astype(BF16))
                oT = lax.dot_general(v_aug, p, _TN, preferred_element_type=F32)
                den = oT[den_row:den_row + 1]
                o_rows.append(oT[hh * DIL_HEAD_DIM:(hh + 1) * DIL_HEAD_DIM] / den)
                lse_rows.append(jnp.broadcast_to(m * LN_2 + jnp.log(den), (DIL_HEAD_DIM, tl)))
            o_ref[rows, sl] = jnp.concatenate(o_rows, axis=0).T.astype(o_ref.dtype)
            lse_ref[rows, sl] = jnp.concatenate(lse_rows, axis=0).T


def _dilated_group(q, k, v, side):
    B, d, L, _ = q.shape
    tl = DIL_Q_TILE
    while tl + 2 * side > L:
        tl //= 2
    rows = min(DIL_TILES_PER_STEP * tl, L)
    q_spec = pl.BlockSpec((None, None, rows, DIL_WIDTH), lambda b, r, n: (b, r, n, 0))
    kv_bytes = L * DIL_WIDTH * 2
    kv_mode = pl.Buffered(1) if 4 * kv_bytes > VMEM_LIMIT_BYTES // 4 else pl.Buffered(2)
    kv_spec = pl.BlockSpec((None, None, L, DIL_WIDTH), lambda b, r, n: (b, r, 0, 0),
                           pipeline_mode=kv_mode)
    return pl.pallas_call(
        functools.partial(_dil_attn_kernel, tl=tl, side=side),
        grid=(B, d, L // rows),
        scratch_shapes=[pltpu.VMEM((2 * (tl + 2 * side), tl), F32)] * 2,
        in_specs=[q_spec, kv_spec, kv_spec],
        out_specs=[q_spec, q_spec],
        out_shape=[jax.ShapeDtypeStruct(q.shape, BF16), jax.ShapeDtypeStruct(q.shape, F32)],
        compiler_params=pltpu.CompilerParams(
            dimension_semantics=("parallel", "parallel", "arbitrary"),
            vmem_limit_bytes=VMEM_LIMIT_BYTES),
        name=f"dil_attn_d{d}",
    )(q, k, v)


def _out_kernel(x_ref, a_ref, o0_ref, o1_ref, o2_ref, l0_ref, l1_ref, l2_ref,
                gattn_ref, wzg_ref, bg_ref, womla_ref, wodil_ref, wout_ref, gfin_ref, out_ref,
                stage_ref, *, final_norm):
    x = x_ref[...]
    tm = x.shape[0]
    hb = _rms(x, gattn_ref[...]).astype(BF16)
    zg = _dot(hb, wzg_ref[...])
    z_mla = zg[:, :MLA_WIDTH]
    z_dil = zg[:, MLA_WIDTH:MLA_WIDTH + DIL_WIDTH]
    g = zg[:, MLA_WIDTH + DIL_WIDTH:] + bg_ref[...]
    g_mla, g_dil = g[:, :D_MODEL], g[:, D_MODEL:]

    a = a_ref[...].astype(F32).T
    y_mla = _dot((a * jax.nn.silu(z_mla)).astype(BF16), womla_ref[...])

    staged = [0]

    def token_order(ref):
        d = ref.shape[0]
        cols = []
        for c in range(DIL_WIDTH // LANES):
            slab = stage_ref.at[staged[0] % stage_ref.shape[0]]
            staged[0] += 1
            for r in range(d):
                slab[pl.ds(r, tm // d, stride=d), :] = ref[r, :, c * LANES:(c + 1) * LANES].astype(F32)
            cols.append(slab[...])
        return jnp.concatenate(cols, axis=1)

    o0 = o0_ref[...].reshape(tm, DIL_WIDTH).astype(F32)
    l0 = l0_ref[...].reshape(tm, DIL_WIDTH)
    o1, l1 = token_order(o1_ref), token_order(l1_ref)
    o2, l2 = token_order(o2_ref), token_order(l2_ref)
    mx = jnp.maximum(jnp.maximum(l0, l1), l2)
    e0, e1, e2 = jnp.exp(l0 - mx), jnp.exp(l1 - mx), jnp.exp(l2 - mx)
    den = e0 + e1 + e2
    d = e0 / den * o0 + e1 / den * o1 + e2 / den * o2
    y_dil = _dot((d * jax.nn.silu(z_dil)).astype(BF16), wodil_ref[...])

    merged = jax.nn.sigmoid(g_mla) * y_mla + jax.nn.sigmoid(g_dil) * y_dil
    y = x + _dot(merged.astype(BF16), wout_ref[...])
    out_ref[...] = _rms(y, gfin_ref[...]) if final_norm else y


def _output(x3, a_out, os, lses, attn_norm_g, w_in, b_gate, w_o_mla, w_o_dil, w_out, final_norm_g, *,
            final_norm):
    B, S, _ = x3.shape
    tm = TOKEN_TILE
    o = _split_offsets()
    w_zg = jnp.concatenate([w_in[:, o[3]:o[4]], w_in[:, o[7]:o[10]]], axis=1).astype(BF16)
    full = lambda a: pl.BlockSpec(a.shape, lambda b, i: (0,) * a.ndim, pipeline_mode=pl.Buffered(1))
    rows = lambda w: pl.BlockSpec((None, tm, w), lambda b, i: (b, i, 0))
    group = lambda d: pl.BlockSpec((None, d, tm // d, DIL_WIDTH), lambda b, i: (b, 0, i, 0))
    groups = [group(d) for _, d in DIL_CONFIGS]
    params = [attn_norm_g.reshape(1, -1), w_zg, b_gate.reshape(1, -1), w_o_mla.astype(BF16),
              w_o_dil.astype(BF16), w_out.astype(BF16), final_norm_g.reshape(1, -1)]
    return pl.pallas_call(
        functools.partial(_out_kernel, final_norm=final_norm),
        grid=(B, S // tm),
        in_specs=[rows(D_MODEL), pl.BlockSpec((None, MLA_WIDTH, tm), lambda b, i: (b, 0, i))]
        + groups + groups + [full(a) for a in params],
        out_specs=rows(D_MODEL),
        out_shape=jax.ShapeDtypeStruct((B, S, D_MODEL), F32),
        scratch_shapes=[pltpu.VMEM((STAGE_SLABS, tm, LANES), F32)],
        compiler_params=pltpu.CompilerParams(dimension_semantics=("parallel", "parallel"),
                                             vmem_limit_bytes=VMEM_LIMIT_BYTES),
        name="out",
    )(x3, a_out, *os, *lses, *params)


def kernel(x, positions, attn_norm_g, w_in, b_gate, mla_q_norm_g, mla_kv_norm_g, w_uq, w_ukv,
           w_o_mla, w_o_dil, w_out, final_norm_g):
    B, S, _ = x.shape
    depth = w_in.shape[0]
    h = x
    pos3 = positions.reshape(B, S, 1)
    for layer in range(depth):
        (qT, k, vT), (qd, kd, vd) = _projection(
            h, pos3, attn_norm_g[layer], w_in[layer], mla_q_norm_g[layer], mla_kv_norm_g[layer],
            w_uq[layer], w_ukv[layer])
        a_out = _mla_attention(qT, k, vT)

        os, lses = [], []
        for g, (window, dilation) in enumerate(DIL_CONFIGS):
            o_g, lse_g = _dilated_group(qd[g], kd[g], vd[g], window // (2 * dilation))
            os.append(o_g)
            lses.append(lse_g)
        h = _output(h, a_out, os, lses, attn_norm_g[layer], w_in[layer], b_gate[layer],
                    w_o_mla[layer], w_o_dil[layer], w_out[layer], final_norm_g,
                    final_norm=(layer == depth - 1))
    return h
```

```python
import functools

import jax
import jax.numpy as jnp
from jax import lax
from jax.experimental import pallas as pl
from jax.experimental.pallas import tpu as pltpu

D_MODEL = 1024
ROPE_THETA = 10000.0
RMS_EPS = 1e-6
NEG_INF = -1e30

MLA_HEADS = 8
MLA_Q_RANK = 256
MLA_KV_RANK = 128
MLA_NOPE_DIM = 64
MLA_ROPE_DIM = 32
MLA_V_DIM = 64
MLA_QK_DIM = MLA_NOPE_DIM + MLA_ROPE_DIM
MLA_WIDTH = MLA_HEADS * MLA_V_DIM

DIL_CONFIGS = ((128, 1), (512, 4), (2048, 16))
DIL_GROUPS = len(DIL_CONFIGS)
DIL_HEADS_PER_GROUP = 8
DIL_HEAD_DIM = 64
DIL_QKV_WIDTH = DIL_GROUPS * DIL_HEADS_PER_GROUP * DIL_HEAD_DIM
DIL_WIDTH = DIL_HEADS_PER_GROUP * DIL_HEAD_DIM

IN_SPLITS = (MLA_Q_RANK, MLA_KV_RANK, MLA_ROPE_DIM, MLA_WIDTH,
             DIL_QKV_WIDTH, DIL_QKV_WIDTH, DIL_QKV_WIDTH, DIL_WIDTH,
             D_MODEL, D_MODEL)

LANES = 128
BF16_ROWS = 16
MLA_HEAD_PAD = LANES
VMEM_LIMIT_BYTES = 56 * 1024 * 1024

TOKEN_TILE = 512
STAGE_SLABS = 8
MLA_Q_TILE = 512
MLA_Q_TILES_PER_STEP = 4
MLA_K_TILE = 1024
DIL_Q_TILE = 256
DIL_TILES_PER_STEP = 4

BF16 = jnp.bfloat16
F32 = jnp.float32
LOG2_E = 1.4426950408889634
LN_2 = 0.6931471805599453

_NT = (((1,), (1,)), ((), ()))
_TN = (((0,), (0,)), ((), ()))


def _dot(a, b):
    return jnp.dot(a, b, preferred_element_type=F32)


def _rms(x, g):
    return x * lax.rsqrt(jnp.mean(x * x, axis=-1, keepdims=True) + RMS_EPS) * g


def _rope_tile(x, cos, sin_signed, first_half, half):
    rot = jnp.where(first_half, pltpu.roll(x, LANES - half, axis=1), pltpu.roll(x, half, axis=1))
    return x * cos + rot * sin_signed


def _split_offsets():
    o = [0]
    for s in IN_SPLITS:
        o.append(o[-1] + s)
    return o


def _proj_kernel(x_ref, pos_ref, gattn_ref, wmla_ref, wdil_ref, gq_ref, gkv_ref,
                 wuq_ref, wuk_ref, wuvT_ref, tab_ref, qT_ref, k_ref, vT_ref, *rest):
    dil_refs, stage_ref = rest[:-1], rest[-1]
    x = x_ref[...]
    tm = x.shape[0]
    hb = _rms(x, gattn_ref[...]).astype(BF16)
    pos = pos_ref[...].astype(F32)

    tab = tab_ref[...]
    lane = lax.broadcasted_iota(jnp.int32, (1, LANES), 1)

    ang_m = pos * tab[0:1, :]
    cos_m, sin_m = jnp.cos(ang_m), jnp.sin(ang_m) * tab[1:2, :]
    first_m = lane < (MLA_NOPE_DIM + MLA_ROPE_DIM // 2)
    q_scale = MLA_QK_DIM ** -0.5 * LOG2_E

    mla = _dot(hb, wmla_ref[...])
    c_q = mla[:, :MLA_Q_RANK]
    c_kv = mla[:, MLA_Q_RANK:MLA_Q_RANK + MLA_KV_RANK]
    kpe = mla[:, MLA_Q_RANK + MLA_KV_RANK:]
    kpe = _rope_tile(kpe, cos_m, sin_m, first_m, MLA_ROPE_DIM // 2)

    cqn = _rms(c_q, gq_ref[...]).astype(BF16)
    ckn = _rms(c_kv, gkv_ref[...]).astype(BF16)
    q = _dot(cqn, wuq_ref[...])
    kn = _dot(ckn, wuk_ref[...])
    vT_ref[...] = lax.dot_general(wuvT_ref[...], ckn, _NT,
                                  preferred_element_type=F32).astype(BF16)
    cos_q, sin_q = cos_m * q_scale, sin_m * q_scale
    for h in range(MLA_HEADS):
        sl = slice(h * MLA_HEAD_PAD, (h + 1) * MLA_HEAD_PAD)
        qh = _rope_tile(q[:, sl], cos_q, sin_q, first_m, MLA_ROPE_DIM // 2)
        qT_ref[sl, :] = qh.T.astype(BF16)
        k_ref[:, sl] = (kn[:, sl] + kpe).astype(BF16)

    ang_d = pos * tab[2:3, :]
    cos_d, sin_d = jnp.cos(ang_d), jnp.sin(ang_d) * tab[3:4, :]
    first_d = (lane % DIL_HEAD_DIM) < (DIL_HEAD_DIM // 2)
    qd_scale = DIL_HEAD_DIM ** -0.5 * LOG2_E
    tiles_per_group = DIL_WIDTH // LANES
    n_staged = 0
    for part in range(3):
        y = _dot(hb, wdil_ref[:, part * DIL_QKV_WIDTH:(part + 1) * DIL_QKV_WIDTH])
        cos_p, sin_p = (cos_d * qd_scale, sin_d * qd_scale) if part == 0 else (cos_d, sin_d)
        for g in range(DIL_GROUPS):
            out_ref = dil_refs[part * DIL_GROUPS + g]
            d = out_ref.shape[0]
            for c in range(tiles_per_group):
                t = g * tiles_per_group + c
                yt = y[:, t * LANES:(t + 1) * LANES]
                if part < 2:
                    yt = _rope_tile(yt, cos_p, sin_p, first_d, DIL_HEAD_DIM // 2)
                lanes = slice(c * LANES, (c + 1) * LANES)
                if d == 1:
                    out_ref[0, :, lanes] = yt.astype(BF16)
                    continue
                slab = stage_ref.at[n_staged % stage_ref.shape[0]]
                n_staged += 1
                slab[...] = yt
                for r in range(d):
                    out_ref[r, :, lanes] = slab[pl.ds(r, tm // d, stride=d), :].astype(BF16)


def _rope_tables():
    lane = jnp.arange(LANES)
    inv_m = ROPE_THETA ** (-jnp.arange(0, MLA_ROPE_DIM, 2, dtype=F32) / MLA_ROPE_DIM)
    inv_d = ROPE_THETA ** (-jnp.arange(0, DIL_HEAD_DIM, 2, dtype=F32) / DIL_HEAD_DIM)
    in_rope = (lane >= MLA_NOPE_DIM) & (lane < MLA_QK_DIM)
    r = lane - MLA_NOPE_DIM
    tab_m = jnp.where(in_rope, inv_m[jnp.clip(r, 0, MLA_ROPE_DIM - 1) % (MLA_ROPE_DIM // 2)], 0.0)
    sgn_m = jnp.where(in_rope, jnp.where(r < MLA_ROPE_DIM // 2, -1.0, 1.0), 0.0)
    e = lane % DIL_HEAD_DIM
    tab_d = inv_d[e % (DIL_HEAD_DIM // 2)]
    sgn_d = jnp.where(e < DIL_HEAD_DIM // 2, -1.0, 1.0)
    zeros = jnp.zeros((LANES,), F32)
    return jnp.stack([tab_m, sgn_m, tab_d, sgn_d, zeros, zeros, zeros, zeros]).astype(F32)


def _projection(x3, pos3, attn_norm_g, w_in, gq, gkv, w_uq, w_ukv):
    B, S, _ = x3.shape
    tm = TOKEN_TILE
    o = _split_offsets()
    w_cq, w_ckv, w_kr = w_in[:, o[0]:o[1]], w_in[:, o[1]:o[2]], w_in[:, o[2]:o[3]]
    kpe_slot = jnp.zeros((D_MODEL, LANES), F32).at[:, MLA_NOPE_DIM:MLA_QK_DIM].set(w_kr)
    w_mla = jnp.concatenate([w_cq, w_ckv, kpe_slot], axis=1).astype(BF16)
    w_dil = w_in[:, o[4]:o[7]].astype(BF16)
    wq = w_uq.reshape(MLA_Q_RANK, MLA_HEADS, MLA_QK_DIM)
    wq = jnp.pad(wq, ((0, 0), (0, 0), (0, MLA_HEAD_PAD - MLA_QK_DIM)))
    wq = wq.reshape(MLA_Q_RANK, MLA_HEADS * MLA_HEAD_PAD).astype(BF16)
    wkv = w_ukv.reshape(MLA_KV_RANK, MLA_HEADS, MLA_NOPE_DIM + MLA_V_DIM)
    wk = jnp.pad(wkv[:, :, :MLA_NOPE_DIM], ((0, 0), (0, 0), (0, MLA_HEAD_PAD - MLA_NOPE_DIM)))
    wk = wk.reshape(MLA_KV_RANK, MLA_HEADS * MLA_HEAD_PAD).astype(BF16)
    wv = wkv[:, :, MLA_NOPE_DIM:].reshape(MLA_KV_RANK, MLA_WIDTH).T.astype(BF16)

    full = lambda a: pl.BlockSpec(a.shape, lambda b, i: (0,) * a.ndim, pipeline_mode=pl.Buffered(1))
    rows = lambda w: pl.BlockSpec((None, tm, w), lambda b, i: (b, i, 0))
    cols = lambda h: pl.BlockSpec((None, h, tm), lambda b, i: (b, 0, i))
    params = [attn_norm_g.reshape(1, -1), w_mla, w_dil, gq.reshape(1, -1), gkv.reshape(1, -1),
              wq, wk, wv, _rope_tables()]
    qk_w = MLA_HEADS * MLA_HEAD_PAD
    out_specs = [cols(qk_w), rows(qk_w), cols(MLA_WIDTH)]
    out_shape = [jax.ShapeDtypeStruct((B, qk_w, S), BF16), jax.ShapeDtypeStruct((B, S, qk_w), BF16),
                 jax.ShapeDtypeStruct((B, MLA_WIDTH, S), BF16)]
    for _ in range(3):
        for _, d in DIL_CONFIGS:
            out_specs.append(pl.BlockSpec((None, d, tm // d, DIL_WIDTH), lambda b, i: (b, 0, i, 0)))
            out_shape.append(jax.ShapeDtypeStruct((B, d, S // d, DIL_WIDTH), BF16))
    outs = pl.pallas_call(
        _proj_kernel,
        grid=(B, S // tm),
        in_specs=[rows(D_MODEL), rows(1)] + [full(a) for a in params],
        out_specs=out_specs,
        out_shape=out_shape,
        scratch_shapes=[pltpu.VMEM((STAGE_SLABS, tm, LANES), F32)],
        compiler_params=pltpu.CompilerParams(dimension_semantics=("parallel", "parallel"),
                                             vmem_limit_bytes=VMEM_LIMIT_BYTES),
        name="proj",
    )(x3, pos3, *params)
    return outs[:3], [outs[3 + p * DIL_GROUPS:3 + (p + 1) * DIL_GROUPS] for p in range(3)]


def _mla_attn_kernel(qT_ref, k_ref, vT_ref, oT_ref, s0_ref, s1_ref, *, tq, tk):
    n_tiles = qT_ref.shape[1] // tq
    nk = k_ref.shape[0] // tk

    def scores(step):
        t, j = divmod(step, nk)
        qT = qT_ref[:, t * tq:(t + 1) * tq]
        return _dot(k_ref[j * tk:(j + 1) * tk, :], qT)

    def update(s_ref, j, carry):
        m, l, acc = carry
        s = s_ref[...]
        m_new = jnp.maximum(m, jnp.max(s, axis=0, keepdims=True))
        alpha = jnp.exp2(m - m_new)
        p = jnp.exp2(s - m_new)
        l = alpha * l + jnp.sum(p, axis=0, keepdims=True)
        acc = alpha * acc + _dot(vT_ref[:, j * tk:(j + 1) * tk], p.astype(BF16))
        return m_new, l, acc

    bufs = (s0_ref, s1_ref)
    s0_ref[...] = scores(0)
    for t in range(n_tiles):
        carry = (jnp.full((1, tq), -jnp.inf, F32), jnp.zeros((1, tq), F32),
                 jnp.zeros((MLA_V_DIM, tq), F32))
        for j in range(nk):
            step = t * nk + j
            if step + 1 < n_tiles * nk:
                bufs[(step + 1) % 2][...] = scores(step + 1)
            carry = update(bufs[step % 2], j, carry)
        _, l, acc = carry
        oT_ref[:, t * tq:(t + 1) * tq] = (acc / l).astype(oT_ref.dtype)


def _mla_attention(qT, k, vT):
    B, _, S = qT.shape
    tq, tk = MLA_Q_TILE, MLA_K_TILE
    cols = MLA_Q_TILES_PER_STEP * tq
    return pl.pallas_call(
        functools.partial(_mla_attn_kernel, tq=tq, tk=tk),
        grid=(B, MLA_HEADS, S // cols),
        scratch_shapes=[pltpu.VMEM((tk, tq), F32), pltpu.VMEM((tk, tq), F32)],
        in_specs=[
            pl.BlockSpec((None, MLA_HEAD_PAD, cols), lambda b, h, i: (b, h, i)),
            pl.BlockSpec((None, S, MLA_HEAD_PAD), lambda b, h, i: (b, 0, h)),
            pl.BlockSpec((None, MLA_V_DIM, S), lambda b, h, i: (b, h, 0)),
        ],
        out_specs=pl.BlockSpec((None, MLA_V_DIM, cols), lambda b, h, i: (b, h, i)),
        out_shape=jax.ShapeDtypeStruct((B, MLA_WIDTH, S), BF16),
        compiler_params=pltpu.CompilerParams(
            dimension_semantics=("parallel", "parallel", "parallel"),
            vmem_limit_bytes=VMEM_LIMIT_BYTES),
        name="mla_attn",
    )(qT, k, vT)


def _dil_attn_kernel(q_ref, k_ref, v_ref, o_ref, lse_ref, s0_ref, s1_ref, *, tl, side):
    L = k_ref.shape[0]
    win = tl + 2 * side
    n_tiles = q_ref.shape[0] // tl
    heads = DIL_HEADS_PER_GROUP
    c = lax.broadcasted_iota(jnp.int32, (win, tl), 0)
    a = lax.broadcasted_iota(jnp.int32, (win, tl), 1)
    lane = lax.broadcasted_iota(jnp.int32, (1, LANES), 1)
    lo = lane < DIL_HEAD_DIM
    lanes_of = lambda pair: slice(pair * LANES, (pair + 1) * LANES)

    tiles = []
    for t in range(n_tiles):
        q0 = (pl.program_id(2) * n_tiles + t) * tl
        ks = pl.multiple_of(jnp.clip(q0 - side, 0, L - win), side)
        bias = jnp.where(jnp.abs((a - c) + (q0 - ks)) <= side, 0.0, NEG_INF)
        tiles.append((slice(t * tl, (t + 1) * tl), ks, bias))

    n_pairs = heads // 2

    def scores(item):
        (rows, ks, bias), pair = tiles[item // n_pairs], item % n_pairs
        qp = q_ref[rows, lanes_of(pair)]
        kp = k_ref[pl.ds(ks, win), lanes_of(pair)]
        zero = jnp.zeros_like(kp)
        k2 = jnp.concatenate([jnp.where(lo, kp, zero), jnp.where(lo, zero, kp)], axis=0)
        s2 = lax.dot_general(k2, qp, _NT, preferred_element_type=F32)
        return s2 + jnp.concatenate([bias, bias], axis=0)

    bufs = (s0_ref, s1_ref)
    s0_ref[...] = scores(0)
    for t, (rows, ks, _) in enumerate(tiles):
        for pair in range(n_pairs):
            sl = lanes_of(pair)
            vp = v_ref[pl.ds(ks, win), sl]
            o_rows, lse_rows = [], []
            item = t * n_pairs + pair
            if item + 1 < n_tiles * n_pairs:
                bufs[(item + 1) % 2][...] = scores(item + 1)
            for hh in range(2):
                s = bufs[item % 2][hh * win:(hh + 1) * win, :]
                m = jnp.max(s, axis=0, keepdims=True)
                p = jnp.exp2(s - m).astype(BF16)
                den_row = (1 - hh) * DIL_HEAD_DIM
                v_aug = jnp.where(lo if hh == 0 else ~lo, vp, (lane == den_row).astype(BF16))
                oT = lax.dot_general(v_aug, p, _TN, preferred_element_type=F32)
                den = oT[den_row:den_row + 1]
                o_rows.append(oT[hh * DIL_HEAD_DIM:(hh + 1) * DIL_HEAD_DIM] / den)
                lse_rows.append(jnp.broadcast_to(m * LN_2 + jnp.log(den), (DIL_HEAD_DIM, tl)))
            o_ref[rows, sl] = jnp.concatenate(o_rows, axis=0).T.astype(o_ref.dtype)
            lse_ref[rows, sl] = jnp.concatenate(lse_rows, axis=0).T


def _dilated_group(q, k, v, side):
    B, d, L, _ = q.shape
    tl = DIL_Q_TILE
    while tl + 2 * side > L:
        tl //= 2
    rows = min(DIL_TILES_PER_STEP * tl, L)
    q_spec = pl.BlockSpec((None, None, rows, DIL_WIDTH), lambda b, r, n: (b, r, n, 0))
    kv_bytes = L * DIL_WIDTH * 2
    kv_mode = pl.Buffered(1) if 4 * kv_bytes > VMEM_LIMIT_BYTES // 4 else pl.Buffered(2)
    kv_spec = pl.BlockSpec((None, None, L, DIL_WIDTH), lambda b, r, n: (b, r, 0, 0),
                           pipeline_mode=kv_mode)
    return pl.pallas_call(
        functools.partial(_dil_attn_kernel, tl=tl, side=side),
        grid=(B, d, L // rows),
        scratch_shapes=[pltpu.VMEM((2 * (tl + 2 * side), tl), F32)] * 2,
        in_specs=[q_spec, kv_spec, kv_spec],
        out_specs=[q_spec, q_spec],
        out_shape=[jax.ShapeDtypeStruct(q.shape, BF16), jax.ShapeDtypeStruct(q.shape, F32)],
        compiler_params=pltpu.CompilerParams(
            dimension_semantics=("parallel", "parallel", "arbitrary"),
            vmem_limit_bytes=VMEM_LIMIT_BYTES),
        name=f"dil_attn_d{d}",
    )(q, k, v)


def _out_kernel(x_ref, a_ref, o0_ref, o1_ref, o2_ref, l0_ref, l1_ref, l2_ref,
                gattn_ref, wzg_ref, bg_ref, womla_ref, wodil_ref, wout_ref, gfin_ref, out_ref,
                stage_ref, *, final_norm):
    x = x_ref[...]
    tm = x.shape[0]
    hb = _rms(x, gattn_ref[...]).astype(BF16)
    zg = _dot(hb, wzg_ref[...])
    z_mla = zg[:, :MLA_WIDTH]
    z_dil = zg[:, MLA_WIDTH:MLA_WIDTH + DIL_WIDTH]
    g = zg[:, MLA_WIDTH + DIL_WIDTH:] + bg_ref[...]
    g_mla, g_dil = g[:, :D_MODEL], g[:, D_MODEL:]

    a = a_ref[...].astype(F32).T
    y_mla = _dot((a * jax.nn.silu(z_mla)).astype(BF16), womla_ref[...])

    staged = [0]

    def token_order(ref):
        d = ref.shape[0]
        cols = []
        for c in range(DIL_WIDTH // LANES):
            slab = stage_ref.at[staged[0] % stage_ref.shape[0]]
            staged[0] += 1
            for r in range(d):
                slab[pl.ds(r, tm // d, stride=d), :] = ref[r, :, c * LANES:(c + 1) * LANES].astype(F32)
            cols.append(slab[...])
        return jnp.concatenate(cols, axis=1)

    o0 = o0_ref[...].reshape(tm, DIL_WIDTH).astype(F32)
    l0 = l0_ref[...].reshape(tm, DIL_WIDTH)
    o1, l1 = token_order(o1_ref), token_order(l1_ref)
    o2, l2 = token_order(o2_ref), token_order(l2_ref)
    mx = jnp.maximum(jnp.maximum(l0, l1), l2)
    e0, e1, e2 = jnp.exp(l0 - mx), jnp.exp(l1 - mx), jnp.exp(l2 - mx)
    den = e0 + e1 + e2
    d = e0 / den * o0 + e1 / den * o1 + e2 / den * o2
    y_dil = _dot((d * jax.nn.silu(z_dil)).astype(BF16), wodil_ref[...])

    merged = jax.nn.sigmoid(g_mla) * y_mla + jax.nn.sigmoid(g_dil) * y_dil
    y = x + _dot(merged.astype(BF16), wout_ref[...])
    out_ref[...] = _rms(y, gfin_ref[...]) if final_norm else y


def _output(x3, a_out, os, lses, attn_norm_g, w_in, b_gate, w_o_mla, w_o_dil, w_out, final_norm_g, *,
            final_norm):
    B, S, _ = x3.shape
    tm = TOKEN_TILE
    o = _split_offsets()
    w_zg = jnp.concatenate([w_in[:, o[3]:o[4]], w_in[:, o[7]:o[10]]], axis=1).astype(BF16)
    full = lambda a: pl.BlockSpec(a.shape, lambda b, i: (0,) * a.ndim, pipeline_mode=pl.Buffered(1))
    rows = lambda w: pl.BlockSpec((None, tm, w), lambda b, i: (b, i, 0))
    group = lambda d: pl.BlockSpec((None, d, tm // d, DIL_WIDTH), lambda b, i: (b, 0, i, 0))
    groups = [group(d) for _, d in DIL_CONFIGS]
    params = [attn_norm_g.reshape(1, -1), w_zg, b_gate.reshape(1, -1), w_o_mla.astype(BF16),
              w_o_dil.astype(BF16), w_out.astype(BF16), final_norm_g.reshape(1, -1)]
    return pl.pallas_call(
        functools.partial(_out_kernel, final_norm=final_norm),
        grid=(B, S // tm),
        in_specs=[rows(D_MODEL), pl.BlockSpec((None, MLA_WIDTH, tm), lambda b, i: (b, 0, i))]
        + groups + groups + [full(a) for a in params],
        out_specs=rows(D_MODEL),
        out_shape=jax.ShapeDtypeStruct((B, S, D_MODEL), F32),
        scratch_shapes=[pltpu.VMEM((STAGE_SLABS, tm, LANES), F32)],
        compiler_params=pltpu.CompilerParams(dimension_semantics=("parallel", "parallel"),
                                             vmem_limit_bytes=VMEM_LIMIT_BYTES),
        name="out",
    )(x3, a_out, *os, *lses, *params)


def kernel(x, positions, attn_norm_g, w_in, b_gate, mla_q_norm_g, mla_kv_norm_g, w_uq, w_ukv,
           w_o_mla, w_o_dil, w_out, final_norm_g):
    B, S, _ = x.shape
    depth = w_in.shape[0]
    h = x
    pos3 = positions.reshape(B, S, 1)
    for layer in range(depth):
        (qT, k, vT), (qd, kd, vd) = _projection(
            h, pos3, attn_norm_g[layer], w_in[layer], mla_q_norm_g[layer], mla_kv_norm_g[layer],
            w_uq[layer], w_ukv[layer])
        a_out = _mla_attention(qT, k, vT)

        os, lses = [], []
        for g, (window, dilation) in enumerate(DIL_CONFIGS):
            o_g, lse_g = _dilated_group(qd[g], kd[g], vd[g], window // (2 * dilation))
            os.append(o_g)
            lses.append(lse_g)
        h = _output(h, a_out, os, lses, attn_norm_g[layer], w_in[layer], b_gate[layer],
                    w_o_mla[layer], w_o_dil[layer], w_out[layer], final_norm_g,
                    final_norm=(layer == depth - 1))
    return h
```

```python
import functools

import jax
import jax.numpy as jnp
from jax import lax
from jax.experimental import pallas as pl
from jax.experimental.pallas import tpu as pltpu

D_MODEL = 1024
ROPE_THETA = 10000.0
RMS_EPS = 1e-6
NEG_INF = -1e30

MLA_HEADS = 8
MLA_Q_RANK = 256
MLA_KV_RANK = 128
MLA_NOPE_DIM = 64
MLA_ROPE_DIM = 32
MLA_V_DIM = 64
MLA_QK_DIM = MLA_NOPE_DIM + MLA_ROPE_DIM
MLA_WIDTH = MLA_HEADS * MLA_V_DIM

DIL_CONFIGS = ((128, 1), (512, 4), (2048, 16))
DIL_GROUPS = len(DIL_CONFIGS)
DIL_HEADS_PER_GROUP = 8
DIL_HEAD_DIM = 64
DIL_QKV_WIDTH = DIL_GROUPS * DIL_HEADS_PER_GROUP * DIL_HEAD_DIM
DIL_WIDTH = DIL_HEADS_PER_GROUP * DIL_HEAD_DIM

IN_SPLITS = (MLA_Q_RANK, MLA_KV_RANK, MLA_ROPE_DIM, MLA_WIDTH,
             DIL_QKV_WIDTH, DIL_QKV_WIDTH, DIL_QKV_WIDTH, DIL_WIDTH,
             D_MODEL, D_MODEL)

LANES = 128
BF16_ROWS = 16
MLA_HEAD_PAD = LANES
VMEM_LIMIT_BYTES = 56 * 1024 * 1024

TOKEN_TILE = 512
STAGE_SLABS = 8
MLA_Q_TILE = 512
MLA_Q_TILES_PER_STEP = 4
MLA_K_TILE = 1024
DIL_Q_TILE = 256
DIL_TILES_PER_STEP = 4

BF16 = jnp.bfloat16
F32 = jnp.float32
LOG2_E = 1.4426950408889634
LN_2 = 0.6931471805599453

_NT = (((1,), (1,)), ((), ()))
_TN = (((0,), (0,)), ((), ()))


def _dot(a, b):
    return jnp.dot(a, b, preferred_element_type=F32)


def _rms(x, g):
    return x * lax.rsqrt(jnp.mean(x * x, axis=-1, keepdims=True) + RMS_EPS) * g


def _rope_tile(x, cos, sin_signed, first_half, half):
    rot = jnp.where(first_half, pltpu.roll(x, LANES - half, axis=1), pltpu.roll(x, half, axis=1))
    return x * cos + rot * sin_signed


def _split_offsets():
    o = [0]
    for s in IN_SPLITS:
        o.append(o[-1] + s)
    return o


def _proj_kernel(x_ref, pos_ref, gattn_ref, wmla_ref, wdil_ref, gq_ref, gkv_ref,
                 wuq_ref, wuk_ref, wuvT_ref, tab_ref, qT_ref, k_ref, vT_ref, *rest):
    dil_refs, stage_ref = rest[:-1], rest[-1]
    x = x_ref[...]
    tm = x.shape[0]
    hb = _rms(x, gattn_ref[...]).astype(BF16)
    pos = pos_ref[...].astype(F32)

    tab = tab_ref[...]
    lane = lax.broadcasted_iota(jnp.int32, (1, LANES), 1)

    ang_m = pos * tab[0:1, :]
    cos_m, sin_m = jnp.cos(ang_m), jnp.sin(ang_m) * tab[1:2, :]
    first_m = lane < (MLA_NOPE_DIM + MLA_ROPE_DIM // 2)
    q_scale = MLA_QK_DIM ** -0.5 * LOG2_E

    mla = _dot(hb, wmla_ref[...])
    c_q = mla[:, :MLA_Q_RANK]
    c_kv = mla[:, MLA_Q_RANK:MLA_Q_RANK + MLA_KV_RANK]
    kpe = mla[:, MLA_Q_RANK + MLA_KV_RANK:]
    kpe = _rope_tile(kpe, cos_m, sin_m, first_m, MLA_ROPE_DIM // 2)

    cqn = _rms(c_q, gq_ref[...]).astype(BF16)
    ckn = _rms(c_kv, gkv_ref[...]).astype(BF16)
    q = _dot(cqn, wuq_ref[...])
    kn = _dot(ckn, wuk_ref[...])
    vT_ref[...] = lax.dot_general(wuvT_ref[...], ckn, _NT,
                                  preferred_element_type=F32).astype(BF16)
    cos_q, sin_q = cos_m * q_scale, sin_m * q_scale
    for h in range(MLA_HEADS):
        sl = slice(h * MLA_HEAD_PAD, (h + 1) * MLA_HEAD_PAD)
        qh = _rope_tile(q[:, sl], cos_q, sin_q, first_m, MLA_ROPE_DIM // 2)
        qT_ref[sl, :] = qh.T.astype(BF16)
        k_ref[:, sl] = (kn[:, sl] + kpe).astype(BF16)

    ang_d = pos * tab[2:3, :]
    cos_d, sin_d = jnp.cos(ang_d), jnp.sin(ang_d) * tab[3:4, :]
    first_d = (lane % DIL_HEAD_DIM) < (DIL_HEAD_DIM // 2)
    qd_scale = DIL_HEAD_DIM ** -0.5 * LOG2_E
    tiles_per_group = DIL_WIDTH // LANES
    n_staged = 0
    for part in range(3):
        y = _dot(hb, wdil_ref[:, part * DIL_QKV_WIDTH:(part + 1) * DIL_QKV_WIDTH])
        cos_p, sin_p = (cos_d * qd_scale, sin_d * qd_scale) if part == 0 else (cos_d, sin_d)
        for g in range(DIL_GROUPS):
            out_ref = dil_refs[part * DIL_GROUPS + g]
            d = out_ref.shape[0]
            for c in range(tiles_per_group):
                t = g * tiles_per_group + c
                yt = y[:, t * LANES:(t + 1) * LANES]
                if part < 2:
                    yt = _rope_tile(yt, cos_p, sin_p, first_d, DIL_HEAD_DIM // 2)
                lanes = slice(c * LANES, (c + 1) * LANES)
                if d == 1:
                    out_ref[0, :, lanes] = yt.astype(BF16)
                    continue
                slab = stage_ref.at[n_staged % stage_ref.shape[0]]
                n_staged += 1
                slab[...] = yt
                for r in range(d):
                    out_ref[r, :, lanes] = slab[pl.ds(r, tm // d, stride=d), :].astype(BF16)


def _rope_tables():
    lane = jnp.arange(LANES)
    inv_m = ROPE_THETA ** (-jnp.arange(0, MLA_ROPE_DIM, 2, dtype=F32) / MLA_ROPE_DIM)
    inv_d = ROPE_THETA ** (-jnp.arange(0, DIL_HEAD_DIM, 2, dtype=F32) / DIL_HEAD_DIM)
    in_rope = (lane >= MLA_NOPE_DIM) & (lane < MLA_QK_DIM)
    r = lane - MLA_NOPE_DIM
    tab_m = jnp.where(in_rope, inv_m[jnp.clip(r, 0, MLA_ROPE_DIM - 1) % (MLA_ROPE_DIM // 2)], 0.0)
    sgn_m = jnp.where(in_rope, jnp.where(r < MLA_ROPE_DIM // 2, -1.0, 1.0), 0.0)
    e = lane % DIL_HEAD_DIM
    tab_d = inv_d[e % (DIL_HEAD_DIM // 2)]
    sgn_d = jnp.where(e < DIL_HEAD_DIM // 2, -1.0, 1.0)
    zeros = jnp.zeros((LANES,), F32)
    return jnp.stack([tab_m, sgn_m, tab_d, sgn_d, zeros, zeros, zeros, zeros]).astype(F32)


def _projection(x3, pos3, attn_norm_g, w_in, gq, gkv, w_uq, w_ukv):
    B, S, _ = x3.shape
    tm = TOKEN_TILE
    o = _split_offsets()
    w_cq, w_ckv, w_kr = w_in[:, o[0]:o[1]], w_in[:, o[1]:o[2]], w_in[:, o[2]:o[3]]
    kpe_slot = jnp.zeros((D_MODEL, LANES), F32).at[:, MLA_NOPE_DIM:MLA_QK_DIM].set(w_kr)
    w_mla = jnp.concatenate([w_cq, w_ckv, kpe_slot], axis=1).astype(BF16)
    w_dil = w_in[:, o[4]:o[7]].astype(BF16)
    wq = w_uq.reshape(MLA_Q_RANK, MLA_HEADS, MLA_QK_DIM)
    wq = jnp.pad(wq, ((0, 0), (0, 0), (0, MLA_HEAD_PAD - MLA_QK_DIM)))
    wq = wq.reshape(MLA_Q_RANK, MLA_HEADS * MLA_HEAD_PAD).astype(BF16)
    wkv = w_ukv.reshape(MLA_KV_RANK, MLA_HEADS, MLA_NOPE_DIM + MLA_V_DIM)
    wk = jnp.pad(wkv[:, :, :MLA_NOPE_DIM], ((0, 0), (0, 0), (0, MLA_HEAD_PAD - MLA_NOPE_DIM)))
    wk = wk.reshape(MLA_KV_RANK, MLA_HEADS * MLA_HEAD_PAD).astype(BF16)
    wv = wkv[:, :, MLA_NOPE_DIM:].reshape(MLA_KV_RANK, MLA_WIDTH).T.astype(BF16)

    full = lambda a: pl.BlockSpec(a.shape, lambda b, i: (0,) * a.ndim, pipeline_mode=pl.Buffered(1))
    rows = lambda w: pl.BlockSpec((None, tm, w), lambda b, i: (b, i, 0))
    cols = lambda h: pl.BlockSpec((None, h, tm), lambda b, i: (b, 0, i))
    params = [attn_norm_g.reshape(1, -1), w_mla, w_dil, gq.reshape(1, -1), gkv.reshape(1, -1),
              wq, wk, wv, _rope_tables()]
    qk_w = MLA_HEADS * MLA_HEAD_PAD
    out_specs = [cols(qk_w), rows(qk_w), cols(MLA_WIDTH)]
    out_shape = [jax.ShapeDtypeStruct((B, qk_w, S), BF16), jax.ShapeDtypeStruct((B, S, qk_w), BF16),
                 jax.ShapeDtypeStruct((B, MLA_WIDTH, S), BF16)]
    for _ in range(3):
        for _, d in DIL_CONFIGS:
            out_specs.append(pl.BlockSpec((None, d, tm // d, DIL_WIDTH), lambda b, i: (b, 0, i, 0)))
            out_shape.append(jax.ShapeDtypeStruct((B, d, S // d, DIL_WIDTH), BF16))
    outs = pl.pallas_call(
        _proj_kernel,
        grid=(B, S // tm),
        in_specs=[rows(D_MODEL), rows(1)] + [full(a) for a in params],
        out_specs=out_specs,
        out_shape=out_shape,
        scratch_shapes=[pltpu.VMEM((STAGE_SLABS, tm, LANES), F32)],
        compiler_params=pltpu.CompilerParams(dimension_semantics=("parallel", "parallel"),
                                             vmem_limit_bytes=VMEM_LIMIT_BYTES),
        name="proj",
    )(x3, pos3, *params)
    return outs[:3], [outs[3 + p * DIL_GROUPS:3 + (p + 1) * DIL_GROUPS] for p in range(3)]


def _mla_attn_kernel(qT_ref, k_ref, vT_ref, oT_ref, s0_ref, s1_ref, *, tq, tk):
    n_tiles = qT_ref.shape[1] // tq
    nk = k_ref.shape[0] // tk

    def scores(step):
        t, j = divmod(step, nk)
        qT = qT_ref[:, t * tq:(t + 1) * tq]
        return _dot(k_ref[j * tk:(j + 1) * tk, :], qT)

    def update(s_ref, j, carry):
        m, l, acc = carry
        s = s_ref[...]
        m_new = jnp.maximum(m, jnp.max(s, axis=0, keepdims=True))
        alpha = jnp.exp2(m - m_new)
        p = jnp.exp2(s - m_new)
        l = alpha * l + jnp.sum(p, axis=0, keepdims=True)
        acc = alpha * acc + _dot(vT_ref[:, j * tk:(j + 1) * tk], p.astype(BF16))
        return m_new, l, acc

    bufs = (s0_ref, s1_ref)
    s0_ref[...] = scores(0)
    for t in range(n_tiles):
        carry = (jnp.full((1, tq), -jnp.inf, F32), jnp.zeros((1, tq), F32),
                 jnp.zeros((MLA_V_DIM, tq), F32))
        for j in range(nk):
            step = t * nk + j
            if step + 1 < n_tiles * nk:
                bufs[(step + 1) % 2][...] = scores(step + 1)
            carry = update(bufs[step % 2], j, carry)
        _, l, acc = carry
        oT_ref[:, t * tq:(t + 1) * tq] = (acc / l).astype(oT_ref.dtype)


def _mla_attention(qT, k, vT):
    B, _, S = qT.shape
    tq, tk = MLA_Q_TILE, MLA_K_TILE
    cols = MLA_Q_TILES_PER_STEP * tq
    return pl.pallas_call(
        functools.partial(_mla_attn_kernel, tq=tq, tk=tk),
        grid=(B, MLA_HEADS, S // cols),
        scratch_shapes=[pltpu.VMEM((tk, tq), F32), pltpu.VMEM((tk, tq), F32)],
        in_specs=[
            pl.BlockSpec((None, MLA_HEAD_PAD, cols), lambda b, h, i: (b, h, i)),
            pl.BlockSpec((None, S, MLA_HEAD_PAD), lambda b, h, i: (b, 0, h)),
            pl.BlockSpec((None, MLA_V_DIM, S), lambda b, h, i: (b, h, 0)),
        ],
        out_specs=pl.BlockSpec((None, MLA_V_DIM, cols), lambda b, h, i: (b, h, i)),
        out_shape=jax.ShapeDtypeStruct((B, MLA_WIDTH, S), BF16),
        compiler_params=pltpu.CompilerParams(
            dimension_semantics=("parallel", "parallel", "parallel"),
            vmem_limit_bytes=VMEM_LIMIT_BYTES),
        name="mla_attn",
    )(qT, k, vT)


def _dil_attn_kernel(q_ref, k_ref, v_ref, o_ref, lse_ref, s0_ref, s1_ref, *, tl, side):
    L = k_ref.shape[0]
    win = tl + 2 * side
    n_tiles = q_ref.shape[0] // tl
    heads = DIL_HEADS_PER_GROUP
    c = lax.broadcasted_iota(jnp.int32, (win, tl), 0)
    a = lax.broadcasted_iota(jnp.int32, (win, tl), 1)
    lane = lax.broadcasted_iota(jnp.int32, (1, LANES), 1)
    lo = lane < DIL_HEAD_DIM
    lanes_of = lambda pair: slice(pair * LANES, (pair + 1) * LANES)

    tiles = []
    for t in range(n_tiles):
        q0 = (pl.program_id(2) * n_tiles + t) * tl
        ks = pl.multiple_of(jnp.clip(q0 - side, 0, L - win), side)
        bias = jnp.where(jnp.abs((a - c) + (q0 - ks)) <= side, 0.0, NEG_INF)
        tiles.append((slice(t * tl, (t + 1) * tl), ks, bias))

    n_pairs = heads // 2

    def scores(item):
        (rows, ks, bias), pair = tiles[item // n_pairs], item % n_pairs
        qp = q_ref[rows, lanes_of(pair)]
        kp = k_ref[pl.ds(ks, win), lanes_of(pair)]
        zero = jnp.zeros_like(kp)
        k2 = jnp.concatenate([jnp.where(lo, kp, zero), jnp.where(lo, zero, kp)], axis=0)
        s2 = lax.dot_general(k2, qp, _NT, preferred_element_type=F32)
        return s2 + jnp.concatenate([bias, bias], axis=0)

    bufs = (s0_ref, s1_ref)
    s0_ref[...] = scores(0)
    for t, (rows, ks, _) in enumerate(tiles):
        for pair in range(n_pairs):
            sl = lanes_of(pair)
            vp = v_ref[pl.ds(ks, win), sl]
            o_rows, lse_rows = [], []
            item = t * n_pairs + pair
            if item + 1 < n_tiles * n_pairs:
                bufs[(item + 1) % 2][...] = scores(item + 1)
            for hh in range(2):
                s = bufs[item % 2][hh * win:(hh + 1) * win, :]
                m = jnp.max(s, axis=0, keepdims=True)
                p = jnp.exp2(s - m).astype(BF16)
                den_row = (1 - hh) * DIL_HEAD_DIM
                v_aug = jnp.where(lo if hh == 0 else ~lo, vp, (lane == den_row).astype(BF16))
                oT = lax.dot_general(v_aug, p, _TN, preferred_element_type=F32)
                den = oT[den_row:den_row + 1]
                o_rows.append(oT[hh * DIL_HEAD_DIM:(hh + 1) * DIL_HEAD_DIM] / den)
                lse_rows.append(jnp.broadcast_to(m * LN_2 + jnp.log(den), (DIL_HEAD_DIM, tl)))
            o_ref[rows, sl] = jnp.concatenate(o_rows, axis=0).T.astype(o_ref.dtype)
            lse_ref[rows, sl] = jnp.concatenate(lse_rows, axis=0).T


def _dilated_group(q, k, v, side):
    B, d, L, _ = q.shape
    tl = DIL_Q_TILE
    while tl + 2 * side > L:
        tl //= 2
    rows = min(DIL_TILES_PER_STEP * tl, L)
    q_spec = pl.BlockSpec((None, None, rows, DIL_WIDTH), lambda b, r, n: (b, r, n, 0))
    kv_bytes = L * DIL_WIDTH * 2
    kv_mode = pl.Buffered(2) if 4 * kv_bytes <= (VMEM_LIMIT_BYTES * 5) // 8 else pl.Buffered(1)
    kv_spec = pl.BlockSpec((None, None, L, DIL_WIDTH), lambda b, r, n: (b, r, 0, 0),
                           pipeline_mode=kv_mode)
    return pl.pallas_call(
        functools.partial(_dil_attn_kernel, tl=tl, side=side),
        grid=(B, d, L // rows),
        scratch_shapes=[pltpu.VMEM((2 * (tl + 2 * side), tl), F32)] * 2,
        in_specs=[q_spec, kv_spec, kv_spec],
        out_specs=[q_spec, q_spec],
        out_shape=[jax.ShapeDtypeStruct(q.shape, BF16), jax.ShapeDtypeStruct(q.shape, F32)],
        compiler_params=pltpu.CompilerParams(
            dimension_semantics=("parallel", "parallel", "arbitrary"),
            vmem_limit_bytes=VMEM_LIMIT_BYTES),
        name=f"dil_attn_d{d}",
    )(q, k, v)


def _out_kernel(x_ref, a_ref, o0_ref, o1_ref, o2_ref, l0_ref, l1_ref, l2_ref,
                gattn_ref, wzg_ref, bg_ref, womla_ref, wodil_ref, wout_ref, gfin_ref, out_ref,
                stage_ref, *, final_norm):
    x = x_ref[...]
    tm = x.shape[0]
    hb = _rms(x, gattn_ref[...]).astype(BF16)
    zg = _dot(hb, wzg_ref[...])
    z_mla = zg[:, :MLA_WIDTH]
    z_dil = zg[:, MLA_WIDTH:MLA_WIDTH + DIL_WIDTH]
    g = zg[:, MLA_WIDTH + DIL_WIDTH:] + bg_ref[...]
    g_mla, g_dil = g[:, :D_MODEL], g[:, D_MODEL:]

    a = a_ref[...].astype(F32).T
    y_mla = _dot((a * jax.nn.silu(z_mla)).astype(BF16), womla_ref[...])

    staged = [0]

    def token_order(ref):
        d = ref.shape[0]
        cols = []
        for c in range(DIL_WIDTH // LANES):
            slab = stage_ref.at[staged[0] % stage_ref.shape[0]]
            staged[0] += 1
            for r in range(d):
                slab[pl.ds(r, tm // d, stride=d), :] = ref[r, :, c * LANES:(c + 1) * LANES].astype(F32)
            cols.append(slab[...])
        return jnp.concatenate(cols, axis=1)

    o0 = o0_ref[...].reshape(tm, DIL_WIDTH).astype(F32)
    l0 = l0_ref[...].reshape(tm, DIL_WIDTH)
    o1, l1 = token_order(o1_ref), token_order(l1_ref)
    o2, l2 = token_order(o2_ref), token_order(l2_ref)
    mx = jnp.maximum(jnp.maximum(l0, l1), l2)
    e0, e1, e2 = jnp.exp(l0 - mx), jnp.exp(l1 - mx), jnp.exp(l2 - mx)
    den = e0 + e1 + e2
    d = e0 / den * o0 + e1 / den * o1 + e2 / den * o2
    y_dil = _dot((d * jax.nn.silu(z_dil)).astype(BF16), wodil_ref[...])

    merged = jax.nn.sigmoid(g_mla) * y_mla + jax.nn.sigmoid(g_dil) * y_dil
    y = x + _dot(merged.astype(BF16), wout_ref[...])
    out_ref[...] = _rms(y, gfin_ref[...]) if final_norm else y


def _output(x3, a_out, os, lses, attn_norm_g, w_in, b_gate, w_o_mla, w_o_dil, w_out, final_norm_g, *,
            final_norm):
    B, S, _ = x3.shape
    tm = TOKEN_TILE
    o = _split_offsets()
    w_zg = jnp.concatenate([w_in[:, o[3]:o[4]], w_in[:, o[7]:o[10]]], axis=1).astype(BF16)
    full = lambda a: pl.BlockSpec(a.shape, lambda b, i: (0,) * a.ndim, pipeline_mode=pl.Buffered(1))
    rows = lambda w: pl.BlockSpec((None, tm, w), lambda b, i: (b, i, 0))
    group = lambda d: pl.BlockSpec((None, d, tm // d, DIL_WIDTH), lambda b, i: (b, 0, i, 0))
    groups = [group(d) for _, d in DIL_CONFIGS]
    params = [attn_norm_g.reshape(1, -1), w_zg, b_gate.reshape(1, -1), w_o_mla.astype(BF16),
              w_o_dil.astype(BF16), w_out.astype(BF16), final_norm_g.reshape(1, -1)]
    return pl.pallas_call(
        functools.partial(_out_kernel, final_norm=final_norm),
        grid=(B, S // tm),
        in_specs=[rows(D_MODEL), pl.BlockSpec((None, MLA_WIDTH, tm), lambda b, i: (b, 0, i))]
        + groups + groups + [full(a) for a in params],
        out_specs=rows(D_MODEL),
        out_shape=jax.ShapeDtypeStruct((B, S, D_MODEL), F32),
        scratch_shapes=[pltpu.VMEM((STAGE_SLABS, tm, LANES), F32)],
        compiler_params=pltpu.CompilerParams(dimension_semantics=("parallel", "parallel"),
                                             vmem_limit_bytes=VMEM_LIMIT_BYTES),
        name="out",
    )(x3, a_out, *os, *lses, *params)


def kernel(x, positions, attn_norm_g, w_in, b_gate, mla_q_norm_g, mla_kv_norm_g, w_uq, w_ukv,
           w_o_mla, w_o_dil, w_out, final_norm_g):
    B, S, _ = x.shape
    depth = w_in.shape[0]
    h = x
    pos3 = positions.reshape(B, S, 1)
    for layer in range(depth):
        (qT, k, vT), (qd, kd, vd) = _projection(
            h, pos3, attn_norm_g[layer], w_in[layer], mla_q_norm_g[layer], mla_kv_norm_g[layer],
            w_uq[layer], w_ukv[layer])
        a_out = _mla_attention(qT, k, vT)

        os, lses = [], []
        for g, (window, dilation) in enumerate(DIL_CONFIGS):
            o_g, lse_g = _dilated_group(qd[g], kd[g], vd[g], window // (2 * dilation))
            os.append(o_g)
            lses.append(lse_g)
        h = _output(h, a_out, os, lses, attn_norm_g[layer], w_in[layer], b_gate[layer],
                    w_o_mla[layer], w_o_dil[layer], w_out[layer], final_norm_g,
                    final_norm=(layer == depth - 1))
    return h
```

```python
import functools

import jax
import jax.numpy as jnp
from jax import lax
from jax.experimental import pallas as pl
from jax.experimental.pallas import tpu as pltpu

D_MODEL = 1024
ROPE_THETA = 10000.0
RMS_EPS = 1e-6
NEG_INF = -1e30

MLA_HEADS = 8
MLA_Q_RANK = 256
MLA_KV_RANK = 128
MLA_NOPE_DIM = 64
MLA_ROPE_DIM = 32
MLA_V_DIM = 64
MLA_QK_DIM = MLA_NOPE_DIM + MLA_ROPE_DIM
MLA_WIDTH = MLA_HEADS * MLA_V_DIM

DIL_CONFIGS = ((128, 1), (512, 4), (2048, 16))
DIL_GROUPS = len(DIL_CONFIGS)
DIL_HEADS_PER_GROUP = 8
DIL_HEAD_DIM = 64
DIL_QKV_WIDTH = DIL_GROUPS * DIL_HEADS_PER_GROUP * DIL_HEAD_DIM
DIL_WIDTH = DIL_HEADS_PER_GROUP * DIL_HEAD_DIM

IN_SPLITS = (MLA_Q_RANK, MLA_KV_RANK, MLA_ROPE_DIM, MLA_WIDTH,
             DIL_QKV_WIDTH, DIL_QKV_WIDTH, DIL_QKV_WIDTH, DIL_WIDTH,
             D_MODEL, D_MODEL)

LANES = 128
BF16_ROWS = 16
MLA_HEAD_PAD = LANES
VMEM_LIMIT_BYTES = 56 * 1024 * 1024

TOKEN_TILE = 512
STAGE_SLABS = 8
MLA_Q_TILE = 512
MLA_Q_TILES_PER_STEP = 4
MLA_K_TILE = 1024
DIL_Q_TILE = 256
DIL_TILES_PER_STEP = 4

BF16 = jnp.bfloat16
F32 = jnp.float32
LOG2_E = 1.4426950408889634
LN_2 = 0.6931471805599453

_NT = (((1,), (1,)), ((), ()))
_TN = (((0,), (0,)), ((), ()))


def _dot(a, b):
    return jnp.dot(a, b, preferred_element_type=F32)


def _rms(x, g):
    return x * lax.rsqrt(jnp.mean(x * x, axis=-1, keepdims=True) + RMS_EPS) * g


def _rope_tile(x, cos, sin_signed, first_half, half):
    rot = jnp.where(first_half, pltpu.roll(x, LANES - half, axis=1), pltpu.roll(x, half, axis=1))
    return x * cos + rot * sin_signed


def _split_offsets():
    o = [0]
    for s in IN_SPLITS:
        o.append(o[-1] + s)
    return o


def _proj_kernel(x_ref, pos_ref, gattn_ref, wmla_ref, wdil_ref, gq_ref, gkv_ref,
                 wuq_ref, wuk_ref, wuvT_ref, tab_ref, qT_ref, k_ref, vT_ref, *rest):
    dil_refs, stage_ref = rest[:-1], rest[-1]
    x = x_ref[...]
    tm = x.shape[0]
    hb = _rms(x, gattn_ref[...]).astype(BF16)
    pos = pos_ref[...].astype(F32)

    tab = tab_ref[...]
    lane = lax.broadcasted_iota(jnp.int32, (1, LANES), 1)

    ang = pos * tab[0:1, :]
    cos_p, sin_p = jnp.cos(ang), jnp.sin(ang)
    n_d, n_m = DIL_HEAD_DIM // 2, MLA_ROPE_DIM // 2

    def mla_table(t, first_sign, fill):
        lo_half = (lane >= MLA_NOPE_DIM) & (lane < MLA_NOPE_DIM + n_m)
        hi_half = (lane >= MLA_NOPE_DIM + n_m) & (lane < MLA_QK_DIM)
        first = pltpu.roll(t, MLA_NOPE_DIM - n_d, axis=1)
        second = pltpu.roll(t, MLA_NOPE_DIM - n_d + n_m, axis=1)
        return jnp.where(lo_half, first * first_sign, jnp.where(hi_half, second, fill))

    def dil_table(t):
        t = jnp.where(lane < n_d, t, pltpu.roll(t, n_d, axis=1))
        return jnp.where(lane < 2 * n_d, t, pltpu.roll(t, 2 * n_d, axis=1))

    cos_m, sin_m = mla_table(cos_p, 1.0, 1.0), mla_table(sin_p, -1.0, 0.0)
    first_m = lane < (MLA_NOPE_DIM + MLA_ROPE_DIM // 2)
    q_scale = MLA_QK_DIM ** -0.5 * LOG2_E

    mla = _dot(hb, wmla_ref[...])
    c_q = mla[:, :MLA_Q_RANK]
    c_kv = mla[:, MLA_Q_RANK:MLA_Q_RANK + MLA_KV_RANK]
    kpe = mla[:, MLA_Q_RANK + MLA_KV_RANK:]
    kpe = _rope_tile(kpe, cos_m, sin_m, first_m, MLA_ROPE_DIM // 2)

    cqn = _rms(c_q, gq_ref[...]).astype(BF16)
    ckn = _rms(c_kv, gkv_ref[...]).astype(BF16)
    q = _dot(cqn, wuq_ref[...])
    kn = _dot(ckn, wuk_ref[...])
    vT_ref[...] = lax.dot_general(wuvT_ref[...], ckn, _NT,
                                  preferred_element_type=F32).astype(BF16)
    cos_q, sin_q = cos_m * q_scale, sin_m * q_scale
    for h in range(MLA_HEADS):
        sl = slice(h * MLA_HEAD_PAD, (h + 1) * MLA_HEAD_PAD)
        qh = _rope_tile(q[:, sl], cos_q, sin_q, first_m, MLA_ROPE_DIM // 2)
        qT_ref[sl, :] = qh.T.astype(BF16)
        k_ref[:, sl] = (kn[:, sl] + kpe).astype(BF16)

    first_d = (lane % DIL_HEAD_DIM) < n_d
    cos_d, sin_d = dil_table(cos_p), dil_table(sin_p) * jnp.where(first_d, -1.0, 1.0)
    qd_scale = DIL_HEAD_DIM ** -0.5 * LOG2_E
    tiles_per_group = DIL_WIDTH // LANES
    n_staged = 0
    for part in range(3):
        y = _dot(hb, wdil_ref[:, part * DIL_QKV_WIDTH:(part + 1) * DIL_QKV_WIDTH])
        cos_r, sin_r = (cos_d * qd_scale, sin_d * qd_scale) if part == 0 else (cos_d, sin_d)
        for g in range(DIL_GROUPS):
            out_ref = dil_refs[part * DIL_GROUPS + g]
            d = out_ref.shape[0]
            for c in range(tiles_per_group):
                t = g * tiles_per_group + c
                yt = y[:, t * LANES:(t + 1) * LANES]
                if part < 2:
                    yt = _rope_tile(yt, cos_r, sin_r, first_d, DIL_HEAD_DIM // 2)
                lanes = slice(c * LANES, (c + 1) * LANES)
                if d == 1:
                    out_ref[0, :, lanes] = yt.astype(BF16)
                    continue
                slab = stage_ref.at[n_staged % stage_ref.shape[0]]
                n_staged += 1
                slab[...] = yt
                for r in range(d):
                    out_ref[r, :, lanes] = slab[pl.ds(r, tm // d, stride=d), :].astype(BF16)


def _rope_tables():
    inv_m = ROPE_THETA ** (-jnp.arange(0, MLA_ROPE_DIM, 2, dtype=F32) / MLA_ROPE_DIM)
    inv_d = ROPE_THETA ** (-jnp.arange(0, DIL_HEAD_DIM, 2, dtype=F32) / DIL_HEAD_DIM)
    row = jnp.zeros((LANES,), F32).at[:DIL_HEAD_DIM // 2].set(inv_d)
    row = row.at[DIL_HEAD_DIM // 2:DIL_HEAD_DIM // 2 + MLA_ROPE_DIM // 2].set(inv_m)
    return jnp.zeros((8, LANES), F32).at[0].set(row)


def _projection(x3, pos3, attn_norm_g, w_in, gq, gkv, w_uq, w_ukv):
    B, S, _ = x3.shape
    tm = TOKEN_TILE
    o = _split_offsets()
    w_cq, w_ckv, w_kr = w_in[:, o[0]:o[1]], w_in[:, o[1]:o[2]], w_in[:, o[2]:o[3]]
    kpe_slot = jnp.zeros((D_MODEL, LANES), F32).at[:, MLA_NOPE_DIM:MLA_QK_DIM].set(w_kr)
    w_mla = jnp.concatenate([w_cq, w_ckv, kpe_slot], axis=1).astype(BF16)
    w_dil = w_in[:, o[4]:o[7]].astype(BF16)
    wq = w_uq.reshape(MLA_Q_RANK, MLA_HEADS, MLA_QK_DIM)
    wq = jnp.pad(wq, ((0, 0), (0, 0), (0, MLA_HEAD_PAD - MLA_QK_DIM)))
    wq = wq.reshape(MLA_Q_RANK, MLA_HEADS * MLA_HEAD_PAD).astype(BF16)
    wkv = w_ukv.reshape(MLA_KV_RANK, MLA_HEADS, MLA_NOPE_DIM + MLA_V_DIM)
    wk = jnp.pad(wkv[:, :, :MLA_NOPE_DIM], ((0, 0), (0, 0), (0, MLA_HEAD_PAD - MLA_NOPE_DIM)))
    wk = wk.reshape(MLA_KV_RANK, MLA_HEADS * MLA_HEAD_PAD).astype(BF16)
    wv = wkv[:, :, MLA_NOPE_DIM:].reshape(MLA_KV_RANK, MLA_WIDTH).T.astype(BF16)

    full = lambda a: pl.BlockSpec(a.shape, lambda b, i: (0,) * a.ndim, pipeline_mode=pl.Buffered(1))
    rows = lambda w: pl.BlockSpec((None, tm, w), lambda b, i: (b, i, 0))
    cols = lambda h: pl.BlockSpec((None, h, tm), lambda b, i: (b, 0, i))
    params = [attn_norm_g.reshape(1, -1), w_mla, w_dil, gq.reshape(1, -1), gkv.reshape(1, -1),
              wq, wk, wv, _rope_tables()]
    qk_w = MLA_HEADS * MLA_HEAD_PAD
    out_specs = [cols(qk_w), rows(qk_w), cols(MLA_WIDTH)]
    out_shape = [jax.ShapeDtypeStruct((B, qk_w, S), BF16), jax.ShapeDtypeStruct((B, S, qk_w), BF16),
                 jax.ShapeDtypeStruct((B, MLA_WIDTH, S), BF16)]
    for _ in range(3):
        for _, d in DIL_CONFIGS:
            out_specs.append(pl.BlockSpec((None, d, tm // d, DIL_WIDTH), lambda b, i: (b, 0, i, 0)))
            out_shape.append(jax.ShapeDtypeStruct((B, d, S // d, DIL_WIDTH), BF16))
    outs = pl.pallas_call(
        _proj_kernel,
        grid=(B, S // tm),
        in_specs=[rows(D_MODEL), rows(1)] + [full(a) for a in params],
        out_specs=out_specs,
        out_shape=out_shape,
        scratch_shapes=[pltpu.VMEM((STAGE_SLABS, tm, LANES), F32)],
        compiler_params=pltpu.CompilerParams(dimension_semantics=("parallel", "parallel"),
                                             vmem_limit_bytes=VMEM_LIMIT_BYTES),
        name="proj",
    )(x3, pos3, *params)
    return outs[:3], [outs[3 + p * DIL_GROUPS:3 + (p + 1) * DIL_GROUPS] for p in range(3)]


def _mla_attn_kernel(qT_ref, k_ref, vT_ref, oT_ref, s0_ref, s1_ref, *, tq, tk):
    n_tiles = qT_ref.shape[1] // tq
    nk = k_ref.shape[0] // tk

    def scores(step):
        t, j = divmod(step, nk)
        qT = qT_ref[:, t * tq:(t + 1) * tq]
        return _dot(k_ref[j * tk:(j + 1) * tk, :], qT)

    def update(s_ref, j, carry):
        m, l, acc = carry
        s = s_ref[...]
        m_new = jnp.maximum(m, jnp.max(s, axis=0, keepdims=True))
        alpha = jnp.exp2(m - m_new)
        p = jnp.exp2(s - m_new)
        l = alpha * l + jnp.sum(p, axis=0, keepdims=True)
        acc = alpha * acc + _dot(vT_ref[:, j * tk:(j + 1) * tk], p.astype(BF16))
        return m_new, l, acc

    bufs = (s0_ref, s1_ref)
    s0_ref[...] = scores(0)
    for t in range(n_tiles):
        carry = (jnp.full((1, tq), -jnp.inf, F32), jnp.zeros((1, tq), F32),
                 jnp.zeros((MLA_V_DIM, tq), F32))
        for j in range(nk):
            step = t * nk + j
            if step + 1 < n_tiles * nk:
                bufs[(step + 1) % 2][...] = scores(step + 1)
            carry = update(bufs[step % 2], j, carry)
        _, l, acc = carry
        oT_ref[:, t * tq:(t + 1) * tq] = (acc / l).astype(oT_ref.dtype)


def _mla_attention(qT, k, vT):
    B, _, S = qT.shape
    tq, tk = MLA_Q_TILE, MLA_K_TILE
    cols = MLA_Q_TILES_PER_STEP * tq
    return pl.pallas_call(
        functools.partial(_mla_attn_kernel, tq=tq, tk=tk),
        grid=(B, MLA_HEADS, S // cols),
        scratch_shapes=[pltpu.VMEM((tk, tq), F32), pltpu.VMEM((tk, tq), F32)],
        in_specs=[
            pl.BlockSpec((None, MLA_HEAD_PAD, cols), lambda b, h, i: (b, h, i)),
            pl.BlockSpec((None, S, MLA_HEAD_PAD), lambda b, h, i: (b, 0, h)),
            pl.BlockSpec((None, MLA_V_DIM, S), lambda b, h, i: (b, h, 0)),
        ],
        out_specs=pl.BlockSpec((None, MLA_V_DIM, cols), lambda b, h, i: (b, h, i)),
        out_shape=jax.ShapeDtypeStruct((B, MLA_WIDTH, S), BF16),
        compiler_params=pltpu.CompilerParams(
            dimension_semantics=("parallel", "parallel", "parallel"),
            vmem_limit_bytes=VMEM_LIMIT_BYTES),
        name="mla_attn",
    )(qT, k, vT)


def _dil_attn_kernel(q_ref, k_ref, v_ref, o_ref, lse_ref, s0_ref, s1_ref, *, tl, side):
    L = k_ref.shape[0]
    win = tl + 2 * side
    n_tiles = q_ref.shape[0] // tl
    heads = DIL_HEADS_PER_GROUP
    c = lax.broadcasted_iota(jnp.int32, (win, tl), 0)
    a = lax.broadcasted_iota(jnp.int32, (win, tl), 1)
    lane = lax.broadcasted_iota(jnp.int32, (1, LANES), 1)
    lo = lane < DIL_HEAD_DIM
    lanes_of = lambda pair: slice(pair * LANES, (pair + 1) * LANES)

    tiles = []
    for t in range(n_tiles):
        q0 = (pl.program_id(2) * n_tiles + t) * tl
        ks = pl.multiple_of(jnp.clip(q0 - side, 0, L - win), side)
        bias = jnp.where(jnp.abs((a - c) + (q0 - ks)) <= side, 0.0, NEG_INF)
        tiles.append((slice(t * tl, (t + 1) * tl), ks, bias))

    n_pairs = heads // 2

    def scores(item):
        (rows, ks, bias), pair = tiles[item // n_pairs], item % n_pairs
        qp = q_ref[rows, lanes_of(pair)]
        kp = k_ref[pl.ds(ks, win), lanes_of(pair)]
        zero = jnp.zeros_like(kp)
        k2 = jnp.concatenate([jnp.where(lo, kp, zero), jnp.where(lo, zero, kp)], axis=0)
        s2 = lax.dot_general(k2, qp, _NT, preferred_element_type=F32)
        return s2 + jnp.concatenate([bias, bias], axis=0)

    bufs = (s0_ref, s1_ref)
    s0_ref[...] = scores(0)
    for t, (rows, ks, _) in enumerate(tiles):
        for pair in range(n_pairs):
            sl = lanes_of(pair)
            vp = v_ref[pl.ds(ks, win), sl]
            o_rows, lse_rows = [], []
            item = t * n_pairs + pair
            if item + 1 < n_tiles * n_pairs:
                bufs[(item + 1) % 2][...] = scores(item + 1)
            for hh in range(2):
                s = bufs[item % 2][hh * win:(hh + 1) * win, :]
                m = jnp.max(s, axis=0, keepdims=True)
                p = jnp.exp2(s - m).astype(BF16)
                den_row = (1 - hh) * DIL_HEAD_DIM
                v_aug = jnp.where(lo if hh == 0 else ~lo, vp, (lane == den_row).astype(BF16))
                oT = lax.dot_general(v_aug, p, _TN, preferred_element_type=F32)
                den = oT[den_row:den_row + 1]
                o_rows.append(oT[hh * DIL_HEAD_DIM:(hh + 1) * DIL_HEAD_DIM] / den)
                lse_rows.append(jnp.broadcast_to(m * LN_2 + jnp.log(den), (DIL_HEAD_DIM, tl)))
            o_ref[rows, sl] = jnp.concatenate(o_rows, axis=0).T.astype(o_ref.dtype)
            lse_ref[rows, sl] = jnp.concatenate(lse_rows, axis=0).T


def _dilated_group(q, k, v, side):
    B, d, L, _ = q.shape
    tl = DIL_Q_TILE
    while tl + 2 * side > L:
        tl //= 2
    rows = min(DIL_TILES_PER_STEP * tl, L)
    q_spec = pl.BlockSpec((None, None, rows, DIL_WIDTH), lambda b, r, n: (b, r, n, 0))
    kv_bytes = L * DIL_WIDTH * 2
    kv_mode = pl.Buffered(2) if 4 * kv_bytes <= (VMEM_LIMIT_BYTES * 5) // 8 else pl.Buffered(1)
    kv_spec = pl.BlockSpec((None, None, L, DIL_WIDTH), lambda b, r, n: (b, r, 0, 0),
                           pipeline_mode=kv_mode)
    return pl.pallas_call(
        functools.partial(_dil_attn_kernel, tl=tl, side=side),
        grid=(B, d, L // rows),
        scratch_shapes=[pltpu.VMEM((2 * (tl + 2 * side), tl), F32)] * 2,
        in_specs=[q_spec, kv_spec, kv_spec],
        out_specs=[q_spec, q_spec],
        out_shape=[jax.ShapeDtypeStruct(q.shape, BF16), jax.ShapeDtypeStruct(q.shape, F32)],
        compiler_params=pltpu.CompilerParams(
            dimension_semantics=("parallel", "parallel", "arbitrary"),
            vmem_limit_bytes=VMEM_LIMIT_BYTES),
        name=f"dil_attn_d{d}",
    )(q, k, v)


def _out_kernel(x_ref, a_ref, o0_ref, o1_ref, o2_ref, l0_ref, l1_ref, l2_ref,
                gattn_ref, wzg_ref, bg_ref, womla_ref, wodil_ref, wout_ref, gfin_ref, out_ref,
                stage_ref, *, final_norm):
    x = x_ref[...]
    tm = x.shape[0]
    hb = _rms(x, gattn_ref[...]).astype(BF16)
    zg = _dot(hb, wzg_ref[...])
    z_mla = zg[:, :MLA_WIDTH]
    z_dil = zg[:, MLA_WIDTH:MLA_WIDTH + DIL_WIDTH]
    g = zg[:, MLA_WIDTH + DIL_WIDTH:] + bg_ref[...]
    g_mla, g_dil = g[:, :D_MODEL], g[:, D_MODEL:]

    a = a_ref[...].astype(F32).T
    y_mla = _dot((a * jax.nn.silu(z_mla)).astype(BF16), womla_ref[...])

    staged = [0]

    def token_order(ref):
        d = ref.shape[0]
        cols = []
        for c in range(DIL_WIDTH // LANES):
            slab = stage_ref.at[staged[0] % stage_ref.shape[0]]
            staged[0] += 1
            for r in range(d):
                slab[pl.ds(r, tm // d, stride=d), :] = ref[r, :, c * LANES:(c + 1) * LANES].astype(F32)
            cols.append(slab[...])
        return jnp.concatenate(cols, axis=1)

    o0 = o0_ref[...].reshape(tm, DIL_WIDTH).astype(F32)
    l0 = l0_ref[...].reshape(tm, DIL_WIDTH)
    o1, l1 = token_order(o1_ref), token_order(l1_ref)
    o2, l2 = token_order(o2_ref), token_order(l2_ref)
    mx = jnp.maximum(jnp.maximum(l0, l1), l2)
    e0, e1, e2 = jnp.exp(l0 - mx), jnp.exp(l1 - mx), jnp.exp(l2 - mx)
    den = e0 + e1 + e2
    d = e0 / den * o0 + e1 / den * o1 + e2 / den * o2
    y_dil = _dot((d * jax.nn.silu(z_dil)).astype(BF16), wodil_ref[...])

    merged = jax.nn.sigmoid(g_mla) * y_mla + jax.nn.sigmoid(g_dil) * y_dil
    y = x + _dot(merged.astype(BF16), wout_ref[...])
    out_ref[...] = _rms(y, gfin_ref[...]) if final_norm else y


def _output(x3, a_out, os, lses, attn_norm_g, w_in, b_gate, w_o_mla, w_o_dil, w_out, final_norm_g, *,
            final_norm):
    B, S, _ = x3.shape
    tm = TOKEN_TILE
    o = _split_offsets()
    w_zg = jnp.concatenate([w_in[:, o[3]:o[4]], w_in[:, o[7]:o[10]]], axis=1).astype(BF16)
    full = lambda a: pl.BlockSpec(a.shape, lambda b, i: (0,) * a.ndim, pipeline_mode=pl.Buffered(1))
    rows = lambda w: pl.BlockSpec((None, tm, w), lambda b, i: (b, i, 0))
    group = lambda d: pl.BlockSpec((None, d, tm // d, DIL_WIDTH), lambda b, i: (b, 0, i, 0))
    groups = [group(d) for _, d in DIL_CONFIGS]
    params = [attn_norm_g.reshape(1, -1), w_zg, b_gate.reshape(1, -1), w_o_mla.astype(BF16),
              w_o_dil.astype(BF16), w_out.astype(BF16), final_norm_g.reshape(1, -1)]
    return pl.pallas_call(
        functools.partial(_out_kernel, final_norm=final_norm),
        grid=(B, S // tm),
        in_specs=[rows(D_MODEL), pl.BlockSpec((None, MLA_WIDTH, tm), lambda b, i: (b, 0, i))]
        + groups + groups + [full(a) for a in params],
        out_specs=rows(D_MODEL),
        out_shape=jax.ShapeDtypeStruct((B, S, D_MODEL), F32),
        scratch_shapes=[pltpu.VMEM((STAGE_SLABS, tm, LANES), F32)],
        compiler_params=pltpu.CompilerParams(dimension_semantics=("parallel", "parallel"),
                                             vmem_limit_bytes=VMEM_LIMIT_BYTES),
        name="out",
    )(x3, a_out, *os, *lses, *params)


def kernel(x, positions, attn_norm_g, w_in, b_gate, mla_q_norm_g, mla_kv_norm_g, w_uq, w_ukv,
           w_o_mla, w_o_dil, w_out, final_norm_g):
    B, S, _ = x.shape
    depth = w_in.shape[0]
    h = x
    pos3 = positions.reshape(B, S, 1)
    for layer in range(depth):
        (qT, k, vT), (qd, kd, vd) = _projection(
            h, pos3, attn_norm_g[layer], w_in[layer], mla_q_norm_g[layer], mla_kv_norm_g[layer],
            w_uq[layer], w_ukv[layer])
        a_out = _mla_attention(qT, k, vT)

        os, lses = [], []
        for g, (window, dilation) in enumerate(DIL_CONFIGS):
            o_g, lse_g = _dilated_group(qd[g], kd[g], vd[g], window // (2 * dilation))
            os.append(o_g)
            lses.append(lse_g)
        h = _output(h, a_out, os, lses, attn_norm_g[layer], w_in[layer], b_gate[layer],
                    w_o_mla[layer], w_o_dil[layer], w_out[layer], final_norm_g,
                    final_norm=(layer == depth - 1))
    return h
```

```python
import functools

import jax
import jax.numpy as jnp
from jax import lax
from jax.experimental import pallas as pl
from jax.experimental.pallas import tpu as pltpu

D_MODEL = 1024
ROPE_THETA = 10000.0
RMS_EPS = 1e-6
NEG_INF = -1e30

MLA_HEADS = 8
MLA_Q_RANK = 256
MLA_KV_RANK = 128
MLA_NOPE_DIM = 64
MLA_ROPE_DIM = 32
MLA_V_DIM = 64
MLA_QK_DIM = MLA_NOPE_DIM + MLA_ROPE_DIM
MLA_WIDTH = MLA_HEADS * MLA_V_DIM

DIL_CONFIGS = ((128, 1), (512, 4), (2048, 16))
DIL_GROUPS = len(DIL_CONFIGS)
DIL_HEADS_PER_GROUP = 8
DIL_HEAD_DIM = 64
DIL_QKV_WIDTH = DIL_GROUPS * DIL_HEADS_PER_GROUP * DIL_HEAD_DIM
DIL_WIDTH = DIL_HEADS_PER_GROUP * DIL_HEAD_DIM

IN_SPLITS = (MLA_Q_RANK, MLA_KV_RANK, MLA_ROPE_DIM, MLA_WIDTH,
             DIL_QKV_WIDTH, DIL_QKV_WIDTH, DIL_QKV_WIDTH, DIL_WIDTH,
             D_MODEL, D_MODEL)

LANES = 128
BF16_ROWS = 16
MLA_HEAD_PAD = LANES
VMEM_LIMIT_BYTES = 56 * 1024 * 1024

TOKEN_TILE = 512
STAGE_SLABS = 8
MLA_Q_TILE = 512
MLA_Q_TILES_PER_STEP = 4
MLA_K_TILE = 1024
DIL_Q_TILE = 256
DIL_TILES_PER_STEP = 8

BF16 = jnp.bfloat16
F32 = jnp.float32
LOG2_E = 1.4426950408889634
LN_2 = 0.6931471805599453

_NT = (((1,), (1,)), ((), ()))
_TN = (((0,), (0,)), ((), ()))


def _dot(a, b):
    return jnp.dot(a, b, preferred_element_type=F32)


def _rms(x, g):
    return x * lax.rsqrt(jnp.mean(x * x, axis=-1, keepdims=True) + RMS_EPS) * g


def _rope_tile(x, cos, sin_signed, first_half, half):
    rot = jnp.where(first_half, pltpu.roll(x, LANES - half, axis=1), pltpu.roll(x, half, axis=1))
    return x * cos + rot * sin_signed


def _split_offsets():
    o = [0]
    for s in IN_SPLITS:
        o.append(o[-1] + s)
    return o


def _proj_kernel(x_ref, pos_ref, gattn_ref, wmla_ref, wdil_ref, gq_ref, gkv_ref,
                 wuq_ref, wuk_ref, wuvT_ref, tab_ref, qT_ref, k_ref, vT_ref, *rest):
    dil_refs, stage_ref = rest[:-1], rest[-1]
    x = x_ref[...]
    tm = x.shape[0]
    hb = _rms(x, gattn_ref[...]).astype(BF16)
    pos = pos_ref[...].astype(F32)

    tab = tab_ref[...]
    lane = lax.broadcasted_iota(jnp.int32, (1, LANES), 1)

    half = tm // 2
    ang = jnp.where(lane < LANES // 2, pos[:half], pos[half:]) * tab[0:1, :]
    cos_h, sin_h = jnp.cos(ang), jnp.sin(ang)
    cos_p = jnp.concatenate([cos_h, pltpu.roll(cos_h, LANES // 2, axis=1)], axis=0)
    sin_p = jnp.concatenate([sin_h, pltpu.roll(sin_h, LANES // 2, axis=1)], axis=0)
    n_d, n_m = DIL_HEAD_DIM // 2, MLA_ROPE_DIM // 2

    def mla_table(t, first_sign, fill):
        lo_half = (lane >= MLA_NOPE_DIM) & (lane < MLA_NOPE_DIM + n_m)
        hi_half = (lane >= MLA_NOPE_DIM + n_m) & (lane < MLA_QK_DIM)
        first = pltpu.roll(t, MLA_NOPE_DIM - n_d, axis=1)
        second = pltpu.roll(t, MLA_NOPE_DIM - n_d + n_m, axis=1)
        return jnp.where(lo_half, first * first_sign, jnp.where(hi_half, second, fill))

    def dil_table(t):
        t = jnp.where(lane < n_d, t, pltpu.roll(t, n_d, axis=1))
        return jnp.where(lane < 2 * n_d, t, pltpu.roll(t, 2 * n_d, axis=1))

    cos_m, sin_m = mla_table(cos_p, 1.0, 1.0), mla_table(sin_p, -1.0, 0.0)
    first_m = lane < (MLA_NOPE_DIM + MLA_ROPE_DIM // 2)
    q_scale = MLA_QK_DIM ** -0.5 * LOG2_E

    mla = _dot(hb, wmla_ref[...])
    c_q = mla[:, :MLA_Q_RANK]
    c_kv = mla[:, MLA_Q_RANK:MLA_Q_RANK + MLA_KV_RANK]
    kpe = mla[:, MLA_Q_RANK + MLA_KV_RANK:]
    kpe = _rope_tile(kpe, cos_m, sin_m, first_m, MLA_ROPE_DIM // 2)

    cqn = _rms(c_q, gq_ref[...]).astype(BF16)
    ckn = _rms(c_kv, gkv_ref[...]).astype(BF16)
    q = _dot(cqn, wuq_ref[...])
    kn = _dot(ckn, wuk_ref[...])
    vT_ref[...] = lax.dot_general(wuvT_ref[...], ckn, _NT,
                                  preferred_element_type=F32).astype(BF16)
    cos_q, sin_q = cos_m * q_scale, sin_m * q_scale
    for h in range(MLA_HEADS):
        sl = slice(h * MLA_HEAD_PAD, (h + 1) * MLA_HEAD_PAD)
        qh = _rope_tile(q[:, sl], cos_q, sin_q, first_m, MLA_ROPE_DIM // 2)
        qT_ref[sl, :] = qh.T.astype(BF16)
        k_ref[:, sl] = (kn[:, sl] + kpe).astype(BF16)

    first_d = (lane % DIL_HEAD_DIM) < n_d
    cos_d, sin_d = dil_table(cos_p), dil_table(sin_p) * jnp.where(first_d, -1.0, 1.0)
    qd_scale = DIL_HEAD_DIM ** -0.5 * LOG2_E
    tiles_per_group = DIL_WIDTH // LANES
    n_staged = 0
    for part in range(3):
        y = _dot(hb, wdil_ref[:, part * DIL_QKV_WIDTH:(part + 1) * DIL_QKV_WIDTH])
        cos_r, sin_r = (cos_d * qd_scale, sin_d * qd_scale) if part == 0 else (cos_d, sin_d)
        for g in range(DIL_GROUPS):
            out_ref = dil_refs[part * DIL_GROUPS + g]
            d = out_ref.shape[0]
            for c in range(tiles_per_group):
                t = g * tiles_per_group + c
                yt = y[:, t * LANES:(t + 1) * LANES]
                if part < 2:
                    yt = _rope_tile(yt, cos_r, sin_r, first_d, DIL_HEAD_DIM // 2)
                lanes = slice(c * LANES, (c + 1) * LANES)
                if d == 1:
                    out_ref[0, :, lanes] = yt.astype(BF16)
                    continue
                slab = stage_ref.at[n_staged % stage_ref.shape[0]]
                n_staged += 1
                slab[...] = yt
                for r in range(d):
                    out_ref[r, :, lanes] = slab[pl.ds(r, tm // d, stride=d), :].astype(BF16)


def _rope_tables():
    inv_m = ROPE_THETA ** (-jnp.arange(0, MLA_ROPE_DIM, 2, dtype=F32) / MLA_ROPE_DIM)
    inv_d = ROPE_THETA ** (-jnp.arange(0, DIL_HEAD_DIM, 2, dtype=F32) / DIL_HEAD_DIM)
    half = jnp.zeros((LANES // 2,), F32).at[:DIL_HEAD_DIM // 2].set(inv_d)
    half = half.at[DIL_HEAD_DIM // 2:DIL_HEAD_DIM // 2 + MLA_ROPE_DIM // 2].set(inv_m)
    return jnp.zeros((8, LANES), F32).at[0].set(jnp.concatenate([half, half]))


def _projection(x3, pos3, attn_norm_g, w_in, gq, gkv, w_uq, w_ukv):
    B, S, _ = x3.shape
    tm = TOKEN_TILE
    o = _split_offsets()
    w_cq, w_ckv, w_kr = w_in[:, o[0]:o[1]], w_in[:, o[1]:o[2]], w_in[:, o[2]:o[3]]
    kpe_slot = jnp.zeros((D_MODEL, LANES), F32).at[:, MLA_NOPE_DIM:MLA_QK_DIM].set(w_kr)
    w_mla = jnp.concatenate([w_cq, w_ckv, kpe_slot], axis=1).astype(BF16)
    w_dil = w_in[:, o[4]:o[7]].astype(BF16)
    wq = w_uq.reshape(MLA_Q_RANK, MLA_HEADS, MLA_QK_DIM)
    wq = jnp.pad(wq, ((0, 0), (0, 0), (0, MLA_HEAD_PAD - MLA_QK_DIM)))
    wq = wq.reshape(MLA_Q_RANK, MLA_HEADS * MLA_HEAD_PAD).astype(BF16)
    wkv = w_ukv.reshape(MLA_KV_RANK, MLA_HEADS, MLA_NOPE_DIM + MLA_V_DIM)
    wk = jnp.pad(wkv[:, :, :MLA_NOPE_DIM], ((0, 0), (0, 0), (0, MLA_HEAD_PAD - MLA_NOPE_DIM)))
    wk = wk.reshape(MLA_KV_RANK, MLA_HEADS * MLA_HEAD_PAD).astype(BF16)
    wv = wkv[:, :, MLA_NOPE_DIM:].reshape(MLA_KV_RANK, MLA_WIDTH).T.astype(BF16)

    full = lambda a: pl.BlockSpec(a.shape, lambda b, i: (0,) * a.ndim, pipeline_mode=pl.Buffered(1))
    rows = lambda w: pl.BlockSpec((None, tm, w), lambda b, i: (b, i, 0))
    cols = lambda h: pl.BlockSpec((None, h, tm), lambda b, i: (b, 0, i))
    params = [attn_norm_g.reshape(1, -1), w_mla, w_dil, gq.reshape(1, -1), gkv.reshape(1, -1),
              wq, wk, wv, _rope_tables()]
    qk_w = MLA_HEADS * MLA_HEAD_PAD
    out_specs = [cols(qk_w), rows(qk_w), cols(MLA_WIDTH)]
    out_shape = [jax.ShapeDtypeStruct((B, qk_w, S), BF16), jax.ShapeDtypeStruct((B, S, qk_w), BF16),
                 jax.ShapeDtypeStruct((B, MLA_WIDTH, S), BF16)]
    for _ in range(3):
        for _, d in DIL_CONFIGS:
            out_specs.append(pl.BlockSpec((None, d, tm // d, DIL_WIDTH), lambda b, i: (b, 0, i, 0)))
            out_shape.append(jax.ShapeDtypeStruct((B, d, S // d, DIL_WIDTH), BF16))
    outs = pl.pallas_call(
        _proj_kernel,
        grid=(B, S // tm),
        in_specs=[rows(D_MODEL), rows(1)] + [full(a) for a in params],
        out_specs=out_specs,
        out_shape=out_shape,
        scratch_shapes=[pltpu.VMEM((STAGE_SLABS, tm, LANES), F32)],
        compiler_params=pltpu.CompilerParams(dimension_semantics=("parallel", "parallel"),
                                             vmem_limit_bytes=VMEM_LIMIT_BYTES),
        name="proj",
    )(x3, pos3, *params)
    return outs[:3], [outs[3 + p * DIL_GROUPS:3 + (p + 1) * DIL_GROUPS] for p in range(3)]


def _mla_attn_kernel(qT_ref, k_ref, vT_ref, oT_ref, s0_ref, s1_ref, *, tq, tk):
    n_tiles = qT_ref.shape[1] // tq
    nk = k_ref.shape[0] // tk

    def scores(step):
        t, j = divmod(step, nk)
        qT = qT_ref[:, t * tq:(t + 1) * tq]
        return _dot(k_ref[j * tk:(j + 1) * tk, :], qT)

    def update(s_ref, j, carry):
        m, l, acc = carry
        s = s_ref[...]
        m_new = jnp.maximum(m, jnp.max(s, axis=0, keepdims=True))
        alpha = jnp.exp2(m - m_new)
        p = jnp.exp2(s - m_new)
        l = alpha * l + jnp.sum(p, axis=0, keepdims=True)
        acc = alpha * acc + _dot(vT_ref[:, j * tk:(j + 1) * tk], p.astype(BF16))
        return m_new, l, acc

    bufs = (s0_ref, s1_ref)
    s0_ref[...] = scores(0)
    for t in range(n_tiles):
        carry = (jnp.full((1, tq), -jnp.inf, F32), jnp.zeros((1, tq), F32),
                 jnp.zeros((MLA_V_DIM, tq), F32))
        for j in range(nk):
            step = t * nk + j
            if step + 1 < n_tiles * nk:
                bufs[(step + 1) % 2][...] = scores(step + 1)
            carry = update(bufs[step % 2], j, carry)
        _, l, acc = carry
        oT_ref[:, t * tq:(t + 1) * tq] = (acc / l).astype(oT_ref.dtype)


def _mla_attention(qT, k, vT):
    B, _, S = qT.shape
    tq, tk = MLA_Q_TILE, MLA_K_TILE
    cols = MLA_Q_TILES_PER_STEP * tq
    return pl.pallas_call(
        functools.partial(_mla_attn_kernel, tq=tq, tk=tk),
        grid=(B, MLA_HEADS, S // cols),
        scratch_shapes=[pltpu.VMEM((tk, tq), F32), pltpu.VMEM((tk, tq), F32)],
        in_specs=[
            pl.BlockSpec((None, MLA_HEAD_PAD, cols), lambda b, h, i: (b, h, i)),
            pl.BlockSpec((None, S, MLA_HEAD_PAD), lambda b, h, i: (b, 0, h)),
            pl.BlockSpec((None, MLA_V_DIM, S), lambda b, h, i: (b, h, 0)),
        ],
        out_specs=pl.BlockSpec((None, MLA_V_DIM, cols), lambda b, h, i: (b, h, i)),
        out_shape=jax.ShapeDtypeStruct((B, MLA_WIDTH, S), BF16),
        compiler_params=pltpu.CompilerParams(
            dimension_semantics=("parallel", "parallel", "parallel"),
            vmem_limit_bytes=VMEM_LIMIT_BYTES),
        name="mla_attn",
    )(qT, k, vT)


def _dil_attn_kernel(q_ref, k_ref, v_ref, o_ref, lse_ref, s0_ref, s1_ref, *, tl, side):
    L = k_ref.shape[0]
    win = tl + 2 * side
    n_tiles = q_ref.shape[0] // tl
    heads = DIL_HEADS_PER_GROUP
    c = lax.broadcasted_iota(jnp.int32, (win, tl), 0)
    a = lax.broadcasted_iota(jnp.int32, (win, tl), 1)
    lane = lax.broadcasted_iota(jnp.int32, (1, LANES), 1)
    lo = lane < DIL_HEAD_DIM
    lanes_of = lambda pair: slice(pair * LANES, (pair + 1) * LANES)

    tiles = []
    for t in range(n_tiles):
        q0 = (pl.program_id(2) * n_tiles + t) * tl
        ks = pl.multiple_of(jnp.clip(q0 - side, 0, L - win), side)
        bias = jnp.where(jnp.abs((a - c) + (q0 - ks)) <= side, 0.0, NEG_INF)
        tiles.append((slice(t * tl, (t + 1) * tl), ks, bias))

    n_pairs = heads // 2

    def scores(item):
        (rows, ks, bias), pair = tiles[item // n_pairs], item % n_pairs
        qp = q_ref[rows, lanes_of(pair)]
        kp = k_ref[pl.ds(ks, win), lanes_of(pair)]
        zero = jnp.zeros_like(kp)
        k2 = jnp.concatenate([jnp.where(lo, kp, zero), jnp.where(lo, zero, kp)], axis=0)
        s2 = lax.dot_general(k2, qp, _NT, preferred_element_type=F32)
        return s2 + jnp.concatenate([bias, bias], axis=0)

    bufs = (s0_ref, s1_ref)
    s0_ref[...] = scores(0)
    for t, (rows, ks, _) in enumerate(tiles):
        for pair in range(n_pairs):
            sl = lanes_of(pair)
            vp = v_ref[pl.ds(ks, win), sl]
            o_rows, lse_rows = [], []
            item = t * n_pairs + pair
            if item + 1 < n_tiles * n_pairs:
                bufs[(item + 1) % 2][...] = scores(item + 1)
            for hh in range(2):
                s = bufs[item % 2][hh * win:(hh + 1) * win, :]
                m = jnp.max(s, axis=0, keepdims=True)
                p = jnp.exp2(s - m).astype(BF16)
                den_row = (1 - hh) * DIL_HEAD_DIM
                v_aug = jnp.where(lo if hh == 0 else ~lo, vp, (lane == den_row).astype(BF16))
                oT = lax.dot_general(v_aug, p, _TN, preferred_element_type=F32)
                den = oT[den_row:den_row + 1]
                o_rows.append(oT[hh * DIL_HEAD_DIM:(hh + 1) * DIL_HEAD_DIM] / den)
                lse_rows.append(jnp.broadcast_to(m * LN_2 + jnp.log(den), (DIL_HEAD_DIM, tl)))
            o_ref[rows, sl] = jnp.concatenate(o_rows, axis=0).T.astype(o_ref.dtype)
            lse_ref[rows, sl] = jnp.concatenate(lse_rows, axis=0).T


def _dilated_group(q, k, v, side):
    B, d, L, _ = q.shape
    tl = DIL_Q_TILE
    while tl + 2 * side > L:
        tl //= 2
    rows = min(DIL_TILES_PER_STEP * tl, L)
    q_spec = pl.BlockSpec((None, None, rows, DIL_WIDTH), lambda b, r, n: (b, r, n, 0))
    kv_bytes = L * DIL_WIDTH * 2
    kv_mode = pl.Buffered(2) if 4 * kv_bytes <= (VMEM_LIMIT_BYTES * 5) // 8 else pl.Buffered(1)
    kv_spec = pl.BlockSpec((None, None, L, DIL_WIDTH), lambda b, r, n: (b, r, 0, 0),
                           pipeline_mode=kv_mode)
    return pl.pallas_call(
        functools.partial(_dil_attn_kernel, tl=tl, side=side),
        grid=(B, d, L // rows),
        scratch_shapes=[pltpu.VMEM((2 * (tl + 2 * side), tl), F32)] * 2,
        in_specs=[q_spec, kv_spec, kv_spec],
        out_specs=[q_spec, q_spec],
        out_shape=[jax.ShapeDtypeStruct(q.shape, BF16), jax.ShapeDtypeStruct(q.shape, F32)],
        compiler_params=pltpu.CompilerParams(
            dimension_semantics=("parallel", "parallel", "arbitrary"),
            vmem_limit_bytes=VMEM_LIMIT_BYTES),
        name=f"dil_attn_d{d}",
    )(q, k, v)


def _out_kernel(x_ref, a_ref, o0_ref, o1_ref, o2_ref, l0_ref, l1_ref, l2_ref,
                gattn_ref, wzg_ref, bg_ref, womla_ref, wodil_ref, wout_ref, gfin_ref, out_ref,
                stage_ref, *, final_norm):
    x = x_ref[...]
    tm = x.shape[0]
    hb = _rms(x, gattn_ref[...]).astype(BF16)
    zg = _dot(hb, wzg_ref[...])
    z_mla = zg[:, :MLA_WIDTH]
    z_dil = zg[:, MLA_WIDTH:MLA_WIDTH + DIL_WIDTH]
    g = zg[:, MLA_WIDTH + DIL_WIDTH:] + bg_ref[...]
    g_mla, g_dil = g[:, :D_MODEL], g[:, D_MODEL:]

    a = a_ref[...].astype(F32).T
    y_mla = _dot((a * jax.nn.silu(z_mla)).astype(BF16), womla_ref[...])

    staged = [0]

    def token_order(ref):
        d = ref.shape[0]
        cols = []
        for c in range(DIL_WIDTH // LANES):
            slab = stage_ref.at[staged[0] % stage_ref.shape[0]]
            staged[0] += 1
            for r in range(d):
                slab[pl.ds(r, tm // d, stride=d), :] = ref[r, :, c * LANES:(c + 1) * LANES].astype(F32)
            cols.append(slab[...])
        return jnp.concatenate(cols, axis=1)

    o0 = o0_ref[...].reshape(tm, DIL_WIDTH).astype(F32)
    l0 = l0_ref[...].reshape(tm, DIL_WIDTH)
    o1, l1 = token_order(o1_ref), token_order(l1_ref)
    o2, l2 = token_order(o2_ref), token_order(l2_ref)
    mx = jnp.maximum(jnp.maximum(l0, l1), l2)
    e0, e1, e2 = jnp.exp(l0 - mx), jnp.exp(l1 - mx), jnp.exp(l2 - mx)
    den = e0 + e1 + e2
    d = e0 / den * o0 + e1 / den * o1 + e2 / den * o2
    y_dil = _dot((d * jax.nn.silu(z_dil)).astype(BF16), wodil_ref[...])

    merged = jax.nn.sigmoid(g_mla) * y_mla + jax.nn.sigmoid(g_dil) * y_dil
    y = x + _dot(merged.astype(BF16), wout_ref[...])
    out_ref[...] = _rms(y, gfin_ref[...]) if final_norm else y


def _output(x3, a_out, os, lses, attn_norm_g, w_in, b_gate, w_o_mla, w_o_dil, w_out, final_norm_g, *,
            final_norm):
    B, S, _ = x3.shape
    tm = TOKEN_TILE
    o = _split_offsets()
    w_zg = jnp.concatenate([w_in[:, o[3]:o[4]], w_in[:, o[7]:o[10]]], axis=1).astype(BF16)
    full = lambda a: pl.BlockSpec(a.shape, lambda b, i: (0,) * a.ndim, pipeline_mode=pl.Buffered(1))
    rows = lambda w: pl.BlockSpec((None, tm, w), lambda b, i: (b, i, 0))
    group = lambda d: pl.BlockSpec((None, d, tm // d, DIL_WIDTH), lambda b, i: (b, 0, i, 0))
    groups = [group(d) for _, d in DIL_CONFIGS]
    params = [attn_norm_g.reshape(1, -1), w_zg, b_gate.reshape(1, -1), w_o_mla.astype(BF16),
              w_o_dil.astype(BF16), w_out.astype(BF16), final_norm_g.reshape(1, -1)]
    return pl.pallas_call(
        functools.partial(_out_kernel, final_norm=final_norm),
        grid=(B, S // tm),
        in_specs=[rows(D_MODEL), pl.BlockSpec((None, MLA_WIDTH, tm), lambda b, i: (b, 0, i))]
        + groups + groups + [full(a) for a in params],
        out_specs=rows(D_MODEL),
        out_shape=jax.ShapeDtypeStruct((B, S, D_MODEL), F32),
        scratch_shapes=[pltpu.VMEM((STAGE_SLABS, tm, LANES), F32)],
        compiler_params=pltpu.CompilerParams(dimension_semantics=("parallel", "parallel"),
                                             vmem_limit_bytes=VMEM_LIMIT_BYTES),
        name="out",
    )(x3, a_out, *os, *lses, *params)


def kernel(x, positions, attn_norm_g, w_in, b_gate, mla_q_norm_g, mla_kv_norm_g, w_uq, w_ukv,
           w_o_mla, w_o_dil, w_out, final_norm_g):
    B, S, _ = x.shape
    depth = w_in.shape[0]
    h = x
    pos3 = positions.reshape(B, S, 1)
    for layer in range(depth):
        (qT, k, vT), (qd, kd, vd) = _projection(
            h, pos3, attn_norm_g[layer], w_in[layer], mla_q_norm_g[layer], mla_kv_norm_g[layer],
            w_uq[layer], w_ukv[layer])
        a_out = _mla_attention(qT, k, vT)

        os, lses = [], []
        for g, (window, dilation) in enumerate(DIL_CONFIGS):
            o_g, lse_g = _dilated_group(qd[g], kd[g], vd[g], window // (2 * dilation))
            os.append(o_g)
            lses.append(lse_g)
        h = _output(h, a_out, os, lses, attn_norm_g[layer], w_in[layer], b_gate[layer],
                    w_o_mla[layer], w_o_dil[layer], w_out[layer], final_norm_g,
                    final_norm=(layer == depth - 1))
    return h
```

```python
import functools

import jax
import jax.numpy as jnp
from jax import lax
from jax.experimental import pallas as pl
from jax.experimental.pallas import tpu as pltpu

D_MODEL = 1024
ROPE_THETA = 10000.0
RMS_EPS = 1e-6
NEG_INF = -1e30

MLA_HEADS = 8
MLA_Q_RANK = 256
MLA_KV_RANK = 128
MLA_NOPE_DIM = 64
MLA_ROPE_DIM = 32
MLA_V_DIM = 64
MLA_QK_DIM = MLA_NOPE_DIM + MLA_ROPE_DIM
MLA_WIDTH = MLA_HEADS * MLA_V_DIM

DIL_CONFIGS = ((128, 1), (512, 4), (2048, 16))
DIL_GROUPS = len(DIL_CONFIGS)
DIL_HEADS_PER_GROUP = 8
DIL_HEAD_DIM = 64
DIL_QKV_WIDTH = DIL_GROUPS * DIL_HEADS_PER_GROUP * DIL_HEAD_DIM
DIL_WIDTH = DIL_HEADS_PER_GROUP * DIL_HEAD_DIM

IN_SPLITS = (MLA_Q_RANK, MLA_KV_RANK, MLA_ROPE_DIM, MLA_WIDTH,
             DIL_QKV_WIDTH, DIL_QKV_WIDTH, DIL_QKV_WIDTH, DIL_WIDTH,
             D_MODEL, D_MODEL)

LANES = 128
BF16_ROWS = 16
MLA_HEAD_PAD = LANES
VMEM_LIMIT_BYTES = 56 * 1024 * 1024

TOKEN_TILE = 512
STAGE_SLABS = 8
MLA_Q_TILE = 512
MLA_Q_TILES_PER_STEP = 4
MLA_K_TILE = 1024
DIL_Q_TILE = 256
DIL_TILES_PER_STEP = 8

BF16 = jnp.bfloat16
F32 = jnp.float32
LOG2_E = 1.4426950408889634
LN_2 = 0.6931471805599453

_NT = (((1,), (1,)), ((), ()))
_TN = (((0,), (0,)), ((), ()))


def _dot(a, b):
    return jnp.dot(a, b, preferred_element_type=F32)


def _rms(x, g):
    return x * lax.rsqrt(jnp.mean(x * x, axis=-1, keepdims=True) + RMS_EPS) * g


def _rope_tile(x, cos, sin_signed, first_half, half):
    rot = jnp.where(first_half, pltpu.roll(x, LANES - half, axis=1), pltpu.roll(x, half, axis=1))
    return x * cos + rot * sin_signed


def _split_offsets():
    o = [0]
    for s in IN_SPLITS:
        o.append(o[-1] + s)
    return o


def _proj_kernel(x_ref, pos_ref, gattn_ref, wmla_ref, wdil_ref, gq_ref, gkv_ref,
                 wuq_ref, wuk_ref, wuvT_ref, tab_ref, qT_ref, k_ref, vT_ref, *rest):
    dil_refs, stage_ref = rest[:-1], rest[-1]
    x = x_ref[...]
    tm = x.shape[0]
    hb = _rms(x, gattn_ref[...]).astype(BF16)
    pos = pos_ref[...].astype(F32)

    tab = tab_ref[...]
    lane = lax.broadcasted_iota(jnp.int32, (1, LANES), 1)

    half = tm // 2
    ang = jnp.where(lane < LANES // 2, pos[:half], pos[half:]) * tab[0:1, :]
    cos_h, sin_h = jnp.cos(ang), jnp.sin(ang)
    cos_p = jnp.concatenate([cos_h, pltpu.roll(cos_h, LANES // 2, axis=1)], axis=0)
    sin_p = jnp.concatenate([sin_h, pltpu.roll(sin_h, LANES // 2, axis=1)], axis=0)
    n_d, n_m = DIL_HEAD_DIM // 2, MLA_ROPE_DIM // 2

    def mla_table(t, first_sign, fill):
        lo_half = (lane >= MLA_NOPE_DIM) & (lane < MLA_NOPE_DIM + n_m)
        hi_half = (lane >= MLA_NOPE_DIM + n_m) & (lane < MLA_QK_DIM)
        first = pltpu.roll(t, MLA_NOPE_DIM - n_d, axis=1)
        second = pltpu.roll(t, MLA_NOPE_DIM - n_d + n_m, axis=1)
        return jnp.where(lo_half, first * first_sign, jnp.where(hi_half, second, fill))

    def dil_table(t):
        t = jnp.where(lane < n_d, t, pltpu.roll(t, n_d, axis=1))
        return jnp.where(lane < 2 * n_d, t, pltpu.roll(t, 2 * n_d, axis=1))

    cos_m, sin_m = mla_table(cos_p, 1.0, 1.0), mla_table(sin_p, -1.0, 0.0)
    first_m = lane < (MLA_NOPE_DIM + MLA_ROPE_DIM // 2)
    q_scale = MLA_QK_DIM ** -0.5 * LOG2_E

    mla = _dot(hb, wmla_ref[...])
    c_q = mla[:, :MLA_Q_RANK]
    c_kv = mla[:, MLA_Q_RANK:MLA_Q_RANK + MLA_KV_RANK]
    kpe = mla[:, MLA_Q_RANK + MLA_KV_RANK:]
    kpe = _rope_tile(kpe, cos_m, sin_m, first_m, MLA_ROPE_DIM // 2)

    cqn = _rms(c_q, gq_ref[...]).astype(BF16)
    ckn = _rms(c_kv, gkv_ref[...]).astype(BF16)
    q = _dot(cqn, wuq_ref[...])
    kn = _dot(ckn, wuk_ref[...])
    vT_ref[...] = lax.dot_general(wuvT_ref[...], ckn, _NT,
                                  preferred_element_type=F32).astype(BF16)
    cos_q, sin_q = cos_m * q_scale, sin_m * q_scale
    for h in range(MLA_HEADS):
        sl = slice(h * MLA_HEAD_PAD, (h + 1) * MLA_HEAD_PAD)
        qh = _rope_tile(q[:, sl], cos_q, sin_q, first_m, MLA_ROPE_DIM // 2)
        qT_ref[sl, :] = qh.T.astype(BF16)
        k_ref[:, sl] = (kn[:, sl] + kpe).astype(BF16)

    first_d = (lane % DIL_HEAD_DIM) < n_d
    cos_d, sin_d = dil_table(cos_p), dil_table(sin_p) * jnp.where(first_d, -1.0, 1.0)
    qd_scale = DIL_HEAD_DIM ** -0.5 * LOG2_E
    tiles_per_group = DIL_WIDTH // LANES
    n_staged = 0
    for part in range(3):
        y = _dot(hb, wdil_ref[:, part * DIL_QKV_WIDTH:(part + 1) * DIL_QKV_WIDTH])
        cos_r, sin_r = (cos_d * qd_scale, sin_d * qd_scale) if part == 0 else (cos_d, sin_d)
        for g in range(DIL_GROUPS):
            out_ref = dil_refs[part * DIL_GROUPS + g]
            d = out_ref.shape[0]
            for c in range(tiles_per_group):
                t = g * tiles_per_group + c
                yt = y[:, t * LANES:(t + 1) * LANES]
                if part < 2:
                    yt = _rope_tile(yt, cos_r, sin_r, first_d, DIL_HEAD_DIM // 2)
                lanes = slice(c * LANES, (c + 1) * LANES)
                if d == 1:
                    out_ref[0, :, lanes] = yt.astype(BF16)
                    continue
                slab = stage_ref.at[n_staged % stage_ref.shape[0]]
                n_staged += 1
                slab[...] = yt
                for r in range(d):
                    out_ref[r, :, lanes] = slab[pl.ds(r, tm // d, stride=d), :].astype(BF16)


def _rope_tables():
    inv_m = ROPE_THETA ** (-jnp.arange(0, MLA_ROPE_DIM, 2, dtype=F32) / MLA_ROPE_DIM)
    inv_d = ROPE_THETA ** (-jnp.arange(0, DIL_HEAD_DIM, 2, dtype=F32) / DIL_HEAD_DIM)
    half = jnp.zeros((LANES // 2,), F32).at[:DIL_HEAD_DIM // 2].set(inv_d)
    half = half.at[DIL_HEAD_DIM // 2:DIL_HEAD_DIM // 2 + MLA_ROPE_DIM // 2].set(inv_m)
    return jnp.zeros((8, LANES), F32).at[0].set(jnp.concatenate([half, half]))


def _projection(x3, pos3, attn_norm_g, w_in, gq, gkv, w_uq, w_ukv):
    B, S, _ = x3.shape
    tm = TOKEN_TILE
    o = _split_offsets()
    w_cq, w_ckv, w_kr = w_in[:, o[0]:o[1]], w_in[:, o[1]:o[2]], w_in[:, o[2]:o[3]]
    kpe_slot = jnp.zeros((D_MODEL, LANES), F32).at[:, MLA_NOPE_DIM:MLA_QK_DIM].set(w_kr)
    w_mla = jnp.concatenate([w_cq, w_ckv, kpe_slot], axis=1).astype(BF16)
    w_dil = w_in[:, o[4]:o[7]].astype(BF16)
    wq = w_uq.reshape(MLA_Q_RANK, MLA_HEADS, MLA_QK_DIM)
    wq = jnp.pad(wq, ((0, 0), (0, 0), (0, MLA_HEAD_PAD - MLA_QK_DIM)))
    wq = wq.reshape(MLA_Q_RANK, MLA_HEADS * MLA_HEAD_PAD).astype(BF16)
    wkv = w_ukv.reshape(MLA_KV_RANK, MLA_HEADS, MLA_NOPE_DIM + MLA_V_DIM)
    wk = jnp.pad(wkv[:, :, :MLA_NOPE_DIM], ((0, 0), (0, 0), (0, MLA_HEAD_PAD - MLA_NOPE_DIM)))
    wk = wk.reshape(MLA_KV_RANK, MLA_HEADS * MLA_HEAD_PAD).astype(BF16)
    wv = wkv[:, :, MLA_NOPE_DIM:].reshape(MLA_KV_RANK, MLA_WIDTH).T.astype(BF16)

    full = lambda a: pl.BlockSpec(a.shape, lambda b, i: (0,) * a.ndim, pipeline_mode=pl.Buffered(1))
    rows = lambda w: pl.BlockSpec((None, tm, w), lambda b, i: (b, i, 0))
    cols = lambda h: pl.BlockSpec((None, h, tm), lambda b, i: (b, 0, i))
    params = [attn_norm_g.reshape(1, -1), w_mla, w_dil, gq.reshape(1, -1), gkv.reshape(1, -1),
              wq, wk, wv, _rope_tables()]
    qk_w = MLA_HEADS * MLA_HEAD_PAD
    out_specs = [cols(qk_w), rows(qk_w), cols(MLA_WIDTH)]
    out_shape = [jax.ShapeDtypeStruct((B, qk_w, S), BF16), jax.ShapeDtypeStruct((B, S, qk_w), BF16),
                 jax.ShapeDtypeStruct((B, MLA_WIDTH, S), BF16)]
    for _ in range(3):
        for _, d in DIL_CONFIGS:
            out_specs.append(pl.BlockSpec((None, d, tm // d, DIL_WIDTH), lambda b, i: (b, 0, i, 0)))
            out_shape.append(jax.ShapeDtypeStruct((B, d, S // d, DIL_WIDTH), BF16))
    outs = pl.pallas_call(
        _proj_kernel,
        grid=(B, S // tm),
        in_specs=[rows(D_MODEL), rows(1)] + [full(a) for a in params],
        out_specs=out_specs,
        out_shape=out_shape,
        scratch_shapes=[pltpu.VMEM((STAGE_SLABS, tm, LANES), F32)],
        compiler_params=pltpu.CompilerParams(dimension_semantics=("parallel", "parallel"),
                                             vmem_limit_bytes=VMEM_LIMIT_BYTES),
        name="proj",
    )(x3, pos3, *params)
    return outs[:3], [outs[3 + p * DIL_GROUPS:3 + (p + 1) * DIL_GROUPS] for p in range(3)]


def _mla_attn_kernel(qT_ref, k_ref, vT_ref, oT_ref, s0_ref, s1_ref, *, tq, tk):
    n_tiles = qT_ref.shape[1] // tq
    nk = k_ref.shape[0] // tk

    def scores(step):
        t, j = divmod(step, nk)
        qT = qT_ref[:, t * tq:(t + 1) * tq]
        return _dot(k_ref[j * tk:(j + 1) * tk, :], qT)

    def update(s_ref, j, carry):
        m, l, acc = carry
        half = tq // 2
        m_parts, sum_parts, p_parts = [], [], []
        for c in range(2):
            cols = slice(c * half, (c + 1) * half)
            s = s_ref[:, cols]
            m_c = jnp.maximum(m[:, cols], jnp.max(s, axis=0, keepdims=True))
            p_c = jnp.exp2(s - m_c)
            m_parts.append(m_c)
            sum_parts.append(jnp.sum(p_c, axis=0, keepdims=True))
            p_parts.append(p_c.astype(BF16))
        m_new = jnp.concatenate(m_parts, axis=1)
        alpha = jnp.exp2(m - m_new)
        l = alpha * l + jnp.concatenate(sum_parts, axis=1)
        p = jnp.concatenate(p_parts, axis=1)
        acc = alpha * acc + _dot(vT_ref[:, j * tk:(j + 1) * tk], p)
        return m_new, l, acc

    bufs = (s0_ref, s1_ref)
    s0_ref[...] = scores(0)
    for t in range(n_tiles):
        carry = (jnp.full((1, tq), -jnp.inf, F32), jnp.zeros((1, tq), F32),
                 jnp.zeros((MLA_V_DIM, tq), F32))
        for j in range(nk):
            step = t * nk + j
            if step + 1 < n_tiles * nk:
                bufs[(step + 1) % 2][...] = scores(step + 1)
            carry = update(bufs[step % 2], j, carry)
        _, l, acc = carry
        oT_ref[:, t * tq:(t + 1) * tq] = (acc / l).astype(oT_ref.dtype)


def _mla_attention(qT, k, vT):
    B, _, S = qT.shape
    tq, tk = MLA_Q_TILE, MLA_K_TILE
    cols = MLA_Q_TILES_PER_STEP * tq
    return pl.pallas_call(
        functools.partial(_mla_attn_kernel, tq=tq, tk=tk),
        grid=(B, MLA_HEADS, S // cols),
        scratch_shapes=[pltpu.VMEM((tk, tq), F32), pltpu.VMEM((tk, tq), F32)],
        in_specs=[
            pl.BlockSpec((None, MLA_HEAD_PAD, cols), lambda b, h, i: (b, h, i)),
            pl.BlockSpec((None, S, MLA_HEAD_PAD), lambda b, h, i: (b, 0, h)),
            pl.BlockSpec((None, MLA_V_DIM, S), lambda b, h, i: (b, h, 0)),
        ],
        out_specs=pl.BlockSpec((None, MLA_V_DIM, cols), lambda b, h, i: (b, h, i)),
        out_shape=jax.ShapeDtypeStruct((B, MLA_WIDTH, S), BF16),
        compiler_params=pltpu.CompilerParams(
            dimension_semantics=("parallel", "parallel", "parallel"),
            vmem_limit_bytes=VMEM_LIMIT_BYTES),
        name="mla_attn",
    )(qT, k, vT)


def _dil_attn_kernel(q_ref, k_ref, v_ref, o_ref, lse_ref, s0_ref, s1_ref, *, tl, side):
    L = k_ref.shape[0]
    win = tl + 2 * side
    n_tiles = q_ref.shape[0] // tl
    heads = DIL_HEADS_PER_GROUP
    c = lax.broadcasted_iota(jnp.int32, (win, tl), 0)
    a = lax.broadcasted_iota(jnp.int32, (win, tl), 1)
    lane = lax.broadcasted_iota(jnp.int32, (1, LANES), 1)
    lo = lane < DIL_HEAD_DIM
    lanes_of = lambda pair: slice(pair * LANES, (pair + 1) * LANES)

    tiles = []
    for t in range(n_tiles):
        q0 = (pl.program_id(2) * n_tiles + t) * tl
        ks = pl.multiple_of(jnp.clip(q0 - side, 0, L - win), side)
        bias = jnp.where(jnp.abs((a - c) + (q0 - ks)) <= side, 0.0, NEG_INF)
        tiles.append((slice(t * tl, (t + 1) * tl), ks, bias))

    n_pairs = heads // 2

    def scores(item):
        (rows, ks, bias), pair = tiles[item // n_pairs], item % n_pairs
        qp = q_ref[rows, lanes_of(pair)]
        kp = k_ref[pl.ds(ks, win), lanes_of(pair)]
        zero = jnp.zeros_like(kp)
        k2 = jnp.concatenate([jnp.where(lo, kp, zero), jnp.where(lo, zero, kp)], axis=0)
        s2 = lax.dot_general(k2, qp, _NT, preferred_element_type=F32)
        return s2 + jnp.concatenate([bias, bias], axis=0)

    bufs = (s0_ref, s1_ref)
    s0_ref[...] = scores(0)
    for t, (rows, ks, _) in enumerate(tiles):
        for pair in range(n_pairs):
            sl = lanes_of(pair)
            vp = v_ref[pl.ds(ks, win), sl]
            o_rows, lse_rows = [], []
            item = t * n_pairs + pair
            if item + 1 < n_tiles * n_pairs:
                bufs[(item + 1) % 2][...] = scores(item + 1)
            for hh in range(2):
                s = bufs[item % 2][hh * win:(hh + 1) * win, :]
                m = jnp.max(s, axis=0, keepdims=True)
                p = jnp.exp2(s - m).astype(BF16)
                den_row = (1 - hh) * DIL_HEAD_DIM
                v_aug = jnp.where(lo if hh == 0 else ~lo, vp, (lane == den_row).astype(BF16))
                oT = lax.dot_general(v_aug, p, _TN, preferred_element_type=F32)
                den = oT[den_row:den_row + 1]
                o_rows.append(oT[hh * DIL_HEAD_DIM:(hh + 1) * DIL_HEAD_DIM] / den)
                lse_rows.append(jnp.broadcast_to(m * LN_2 + jnp.log(den), (DIL_HEAD_DIM, tl)))
            o_ref[rows, sl] = jnp.concatenate(o_rows, axis=0).T.astype(o_ref.dtype)
            lse_ref[rows, sl] = jnp.concatenate(lse_rows, axis=0).T


def _dilated_group(q, k, v, side):
    B, d, L, _ = q.shape
    tl = DIL_Q_TILE
    while tl + 2 * side > L:
        tl //= 2
    rows = min(DIL_TILES_PER_STEP * tl, L)
    q_spec = pl.BlockSpec((None, None, rows, DIL_WIDTH), lambda b, r, n: (b, r, n, 0))
    kv_bytes = L * DIL_WIDTH * 2
    kv_mode = pl.Buffered(2) if 4 * kv_bytes <= (VMEM_LIMIT_BYTES * 5) // 8 else pl.Buffered(1)
    kv_spec = pl.BlockSpec((None, None, L, DIL_WIDTH), lambda b, r, n: (b, r, 0, 0),
                           pipeline_mode=kv_mode)
    return pl.pallas_call(
        functools.partial(_dil_attn_kernel, tl=tl, side=side),
        grid=(B, d, L // rows),
        scratch_shapes=[pltpu.VMEM((2 * (tl + 2 * side), tl), F32)] * 2,
        in_specs=[q_spec, kv_spec, kv_spec],
        out_specs=[q_spec, q_spec],
        out_shape=[jax.ShapeDtypeStruct(q.shape, BF16), jax.ShapeDtypeStruct(q.shape, F32)],
        compiler_params=pltpu.CompilerParams(
            dimension_semantics=("parallel", "parallel", "arbitrary"),
            vmem_limit_bytes=VMEM_LIMIT_BYTES),
        name=f"dil_attn_d{d}",
    )(q, k, v)


def _out_kernel(x_ref, a_ref, o0_ref, o1_ref, o2_ref, l0_ref, l1_ref, l2_ref,
                gattn_ref, wzg_ref, bg_ref, womla_ref, wodil_ref, wout_ref, gfin_ref, out_ref,
                stage_ref, *, final_norm):
    x = x_ref[...]
    tm = x.shape[0]
    hb = _rms(x, gattn_ref[...]).astype(BF16)
    zg = _dot(hb, wzg_ref[...])
    z_mla = zg[:, :MLA_WIDTH]
    z_dil = zg[:, MLA_WIDTH:MLA_WIDTH + DIL_WIDTH]
    g = zg[:, MLA_WIDTH + DIL_WIDTH:] + bg_ref[...]
    g_mla, g_dil = g[:, :D_MODEL], g[:, D_MODEL:]

    a = a_ref[...].astype(F32).T
    y_mla = _dot((a * jax.nn.silu(z_mla)).astype(BF16), womla_ref[...])

    staged = [0]

    def token_order(ref):
        d = ref.shape[0]
        cols = []
        for c in range(DIL_WIDTH // LANES):
            slab = stage_ref.at[staged[0] % stage_ref.shape[0]]
            staged[0] += 1
            for r in range(d):
                slab[pl.ds(r, tm // d, stride=d), :] = ref[r, :, c * LANES:(c + 1) * LANES].astype(F32)
            cols.append(slab[...])
        return jnp.concatenate(cols, axis=1)

    o0 = o0_ref[...].reshape(tm, DIL_WIDTH).astype(F32)
    l0 = l0_ref[...].reshape(tm, DIL_WIDTH)
    o1, l1 = token_order(o1_ref), token_order(l1_ref)
    o2, l2 = token_order(o2_ref), token_order(l2_ref)
    mx = jnp.maximum(jnp.maximum(l0, l1), l2)
    e0, e1, e2 = jnp.exp(l0 - mx), jnp.exp(l1 - mx), jnp.exp(l2 - mx)
    den = e0 + e1 + e2
    d = e0 / den * o0 + e1 / den * o1 + e2 / den * o2
    y_dil = _dot((d * jax.nn.silu(z_dil)).astype(BF16), wodil_ref[...])

    merged = jax.nn.sigmoid(g_mla) * y_mla + jax.nn.sigmoid(g_dil) * y_dil
    y = x + _dot(merged.astype(BF16), wout_ref[...])
    out_ref[...] = _rms(y, gfin_ref[...]) if final_norm else y


def _output(x3, a_out, os, lses, attn_norm_g, w_in, b_gate, w_o_mla, w_o_dil, w_out, final_norm_g, *,
            final_norm):
    B, S, _ = x3.shape
    tm = TOKEN_TILE
    o = _split_offsets()
    w_zg = jnp.concatenate([w_in[:, o[3]:o[4]], w_in[:, o[7]:o[10]]], axis=1).astype(BF16)
    full = lambda a: pl.BlockSpec(a.shape, lambda b, i: (0,) * a.ndim, pipeline_mode=pl.Buffered(1))
    rows = lambda w: pl.BlockSpec((None, tm, w), lambda b, i: (b, i, 0))
    group = lambda d: pl.BlockSpec((None, d, tm // d, DIL_WIDTH), lambda b, i: (b, 0, i, 0))
    groups = [group(d) for _, d in DIL_CONFIGS]
    params = [attn_norm_g.reshape(1, -1), w_zg, b_gate.reshape(1, -1), w_o_mla.astype(BF16),
              w_o_dil.astype(BF16), w_out.astype(BF16), final_norm_g.reshape(1, -1)]
    return pl.pallas_call(
        functools.partial(_out_kernel, final_norm=final_norm),
        grid=(B, S // tm),
        in_specs=[rows(D_MODEL), pl.BlockSpec((None, MLA_WIDTH, tm), lambda b, i: (b, 0, i))]
        + groups + groups + [full(a) for a in params],
        out_specs=rows(D_MODEL),
        out_shape=jax.ShapeDtypeStruct((B, S, D_MODEL), F32),
        scratch_shapes=[pltpu.VMEM((STAGE_SLABS, tm, LANES), F32)],
        compiler_params=pltpu.CompilerParams(dimension_semantics=("parallel", "parallel"),
                                             vmem_limit_bytes=VMEM_LIMIT_BYTES),
        name="out",
    )(x3, a_out, *os, *lses, *params)


def kernel(x, positions, attn_norm_g, w_in, b_gate, mla_q_norm_g, mla_kv_norm_g, w_uq, w_ukv,
           w_o_mla, w_o_dil, w_out, final_norm_g):
    B, S, _ = x.shape
    depth = w_in.shape[0]
    h = x
    pos3 = positions.reshape(B, S, 1)
    for layer in range(depth):
        (qT, k, vT), (qd, kd, vd) = _projection(
            h, pos3, attn_norm_g[layer], w_in[layer], mla_q_norm_g[layer], mla_kv_norm_g[layer],
            w_uq[layer], w_ukv[layer])
        a_out = _mla_attention(qT, k, vT)

        os, lses = [], []
        for g, (window, dilation) in enumerate(DIL_CONFIGS):
            o_g, lse_g = _dilated_group(qd[g], kd[g], vd[g], window // (2 * dilation))
            os.append(o_g)
            lses.append(lse_g)
        h = _output(h, a_out, os, lses, attn_norm_g[layer], w_in[layer], b_gate[layer],
                    w_o_mla[layer], w_o_dil[layer], w_out[layer], final_norm_g,
                    final_norm=(layer == depth - 1))
    return h
```
